```python
import math
import jax, jax.numpy as jnp
from jax import lax
import numpy as np

D_MODEL = 1024
BATCH = 8
SEQ = 2048
DEPTH = 2
DEC_BATCH = 32
DEC_SEQ = 1
PAST_LEN = 16384
PAGE_SIZE = 128

N_META = 16
N_A = DEPTH // 2
N_B = DEPTH - N_A
POOL_WINDOWS = (2, 4, 8, 16)
N_POOL_GROUPS = len(POOL_WINDOWS)
POOL_GROUP = D_MODEL // N_POOL_GROUPS
MAX_WIN = max(POOL_WINDOWS)
POOL_CTX = MAX_WIN - 1
HEAD_DIM = 64
V_DIM = 2 * HEAD_DIM
N_HEADS = D_MODEL // V_DIM
QK_WIDTH = N_HEADS * 2 * HEAD_DIM
V_WIDTH = N_HEADS * V_DIM
Q_BLOCK = 128
PEER_HEADS = 8
PEER_KEYS = 128
PEER_EXPERTS = PEER_KEYS * PEER_KEYS
PEER_TOPK = 16
PEER_QDIM = 256
PEER_HALF = PEER_QDIM // 2
PEER_BLOCK = 128
EPS = 1e-6
ALIBI_SLOPES = tuple(2.0 ** (-8.0 * (h + 1) / N_HEADS) for h in range(N_HEADS))
LAMBDA_INIT = tuple(0.8 - 0.6 * math.exp(-0.3 * l) for l in range(N_A, DEPTH))

kernel_name = "yoco_pool_diffattn_peer_step"


def rmsnorm(x, g):
    xf = x.astype(jnp.float32)
    y = xf * lax.rsqrt(jnp.mean(xf * xf, axis=-1, keepdims=True) + EPS)
    return (y * g.astype(jnp.float32)).astype(x.dtype)


def pool_mix(h, p0, n_ctx, w_pool, scale):
    B, L, D = h.shape
    hf = h.astype(jnp.float32)
    csp = jnp.cumsum(jnp.pad(hf, ((0, 0), (MAX_WIN, 0), (0, 0))), axis=1)
    cnt_pos = p0 + jnp.arange(L) + 1
    outs = []
    for g, w in enumerate(POOL_WINDOWS):
        sl = slice(g * POOL_GROUP, (g + 1) * POOL_GROUP)
        win = csp[:, MAX_WIN:, sl] - csp[:, MAX_WIN - w:MAX_WIN - w + L, sl]
        cnt = jnp.minimum(cnt_pos, w).astype(jnp.float32)[None, :, None]
        outs.append(win / cnt - hf[:, :, sl])
    d = jnp.stack(outs, axis=2)[:, n_ctx:]
    y = jnp.einsum('btgc,gce->btge', d, w_pool.astype(jnp.float32)).reshape(B, L - n_ctx, D)
    return (y * scale.astype(jnp.float32)).astype(h.dtype)


def shared_kv(x, g, w_k, w_v):
    B, T, _ = x.shape
    h = rmsnorm(x, g)
    k = (h @ w_k).reshape(B, T, N_HEADS, 2, HEAD_DIM)
    v = (h @ w_v).reshape(B, T, N_HEADS, V_DIM)
    return k, v


def diff_lambda(lam, lam_init):
    lf = lam.astype(jnp.float32)
    return jnp.exp(jnp.sum(lf[0] * lf[1])) - jnp.exp(jnp.sum(lf[2] * lf[3])) + lam_init


def alibi_scores(q, k, qpos, kpos):
    slopes = jnp.asarray(ALIBI_SLOPES, jnp.float32)
    s = jnp.einsum('bqhcd,bkhcd->bhcqk', q, k).astype(jnp.float32) * (HEAD_DIM ** -0.5)
    dist = (qpos[:, None] - kpos[None, :]).astype(jnp.float32)
    s = s - slopes[None, :, None, None, None] * dist[None, None, None]
    return jnp.where((dist >= 0)[None, None, None], s, -jnp.inf)


def diff_attn_prompt(q, k, v, lam):
    B, T = q.shape[:2]
    nb = -(-T // Q_BLOCK)
    Tp = nb * Q_BLOCK
    qp = jnp.pad(q, ((0, 0), (0, Tp - T), (0, 0), (0, 0), (0, 0)))
    qb = qp.reshape(B, nb, Q_BLOCK, N_HEADS, 2, HEAD_DIM).transpose(1, 0, 2, 3, 4, 5)
    kpos = jnp.arange(T)

    def one(args):
        qblk, start = args
        qpos = start + jnp.arange(Q_BLOCK)
        p = jax.nn.softmax(alibi_scores(qblk, k, qpos, kpos), axis=-1)
        a = p[:, :, 0] - lam * p[:, :, 1]
        return jnp.einsum('bhqk,bkhd->bqhd', a.astype(v.dtype), v)

    o = lax.map(one, (qb, jnp.arange(nb) * Q_BLOCK))
    return o.transpose(1, 0, 2, 3, 4).reshape(B, Tp, N_HEADS, V_DIM)[:, :T]


def diff_attn_sample(q, k_new, v_new, cache_k, cache_v, page_table, lam):
    DB, S = q.shape[:2]
    qpos = PAST_LEN + jnp.arange(S)

    def update(carry, s, vb):
        m, l, acc = carry
        m_new = jnp.maximum(m, jnp.max(s, axis=-1))
        corr = jnp.exp(m - m_new)
        p = jnp.exp(s - m_new[..., None])
        l = l * corr + jnp.sum(p, axis=-1)
        acc = acc * corr[..., None] + jnp.einsum('bhcqk,bkhd->bhcqd', p, vb.astype(jnp.float32))
        return (m_new, l, acc)

    def step(carry, xs):
        pt_col, j = xs
        kb = cache_k[pt_col].reshape(DB, PAGE_SIZE, N_HEADS, 2, HEAD_DIM)
        vb = cache_v[pt_col]
        kpos = j * PAGE_SIZE + jnp.arange(PAGE_SIZE)
        return update(carry, alibi_scores(q, kb, qpos, kpos), vb), None

    init = (jnp.full((DB, N_HEADS, 2, S), -jnp.inf, jnp.float32),
            jnp.zeros((DB, N_HEADS, 2, S), jnp.float32),
            jnp.zeros((DB, N_HEADS, 2, S, V_DIM), jnp.float32))
    n_pages = page_table.shape[1]
    carry, _ = lax.scan(step, init, (page_table.T, jnp.arange(n_pages)))
    m, l, acc = update(carry, alibi_scores(q, k_new, qpos, qpos), v_new)
    o = acc / l[..., None]
    out = o[:, :, 0] - lam * o[:, :, 1]
    return out.transpose(0, 2, 1, 3).astype(v_new.dtype)


def diff_out(o, subln_g, lam_init, w_o):
    B, T = o.shape[:2]
    o = rmsnorm(o, subln_g) * (1.0 - lam_init)
    return o.reshape(B, T, V_WIDTH).astype(w_o.dtype) @ w_o


def peer(h, wq, subkeys, u, v):
    shp = h.shape
    hf = h.reshape(-1, D_MODEL)
    n = hf.shape[0]
    blk = min(PEER_BLOCK, n)
    nb = -(-n // blk)
    blocks = jnp.pad(hf, ((0, nb * blk - n), (0, 0))).reshape(nb, blk, D_MODEL)

    def one(hb):
        q = (hb @ wq).reshape(blk, PEER_HEADS, 2, PEER_HALF)
        s = jnp.einsum('nhcd,hckd->nhck', q, subkeys).astype(jnp.float32)
        s1, i1 = lax.top_k(s[:, :, 0], PEER_TOPK)
        s2, i2 = lax.top_k(s[:, :, 1], PEER_TOPK)
        cand = (s1[..., :, None] + s2[..., None, :]).reshape(blk, PEER_HEADS, PEER_TOPK * PEER_TOPK)
        cidx = (i1[..., :, None] * PEER_KEYS + i2[..., None, :]).reshape(blk, PEER_HEADS, PEER_TOPK * PEER_TOPK)
        top, pos = lax.top_k(cand, PEER_TOPK)
        idx = jnp.take_along_axis(cidx, pos, axis=-1)
        gate = jax.nn.softmax(top, axis=-1)
        ue = u[idx]
        ve = v[idx]
        act = jax.nn.gelu(jnp.einsum('nd,nhkd->nhk', hb, ue).astype(jnp.float32))
        return jnp.einsum('nhk,nhkd->nd', (gate * act).astype(ve.dtype), ve)

    y = lax.map(one, blocks).reshape(nb * blk, D_MODEL)[:n]
    return y.reshape(shp).astype(h.dtype)


def setup_inputs(seed: int = 0) -> dict:
    key = jax.random.key(seed)
    ks = jax.random.split(key, 24)
    f32 = jnp.float32
    nrm = lambda k, shape, s: jax.random.normal(k, shape, f32) * s
    n_pages = PAST_LEN // PAGE_SIZE
    n_used = DEC_BATCH * n_pages
    n_pool_pages = (n_used * 5) // 4
    page_table = jax.random.permutation(ks[5], n_pool_pages)[:n_used].reshape(DEC_BATCH, n_pages).astype(jnp.int32)
    return {
        "x_prompt": nrm(ks[0], (BATCH, SEQ, D_MODEL), 1.0),
        "x_sample": nrm(ks[1], (DEC_BATCH, DEC_SEQ, D_MODEL), 1.0),
        "state_pool": nrm(ks[2], (N_A, DEC_BATCH, POOL_CTX, D_MODEL), 1.0),
        "cache_k": nrm(ks[3], (n_pool_pages, PAGE_SIZE, N_HEADS, 2 * HEAD_DIM), 1.0),
        "cache_v": nrm(ks[4], (n_pool_pages, PAGE_SIZE, N_HEADS, V_DIM), 1.0),
        "page_table": page_table,
        "meta_tokens": nrm(ks[6], (N_META, D_MODEL), 1.0),
        "pool_norm": 1.0 + nrm(ks[7], (N_A, D_MODEL), 0.05),
        "pool_w": nrm(ks[8], (N_A, N_POOL_GROUPS, POOL_GROUP, POOL_GROUP), POOL_GROUP ** -0.5),
        "pool_scale": 1.0 + nrm(ks[9], (N_A, D_MODEL), 0.1),
        "kv_norm": 1.0 + nrm(ks[10], (D_MODEL,), 0.05),
        "w_k": nrm(ks[11], (D_MODEL, QK_WIDTH), D_MODEL ** -0.5),
        "w_v": nrm(ks[12], (D_MODEL, V_WIDTH), D_MODEL ** -0.5),
        "attn_norm": 1.0 + nrm(ks[13], (N_B, D_MODEL), 0.05),
        "w_q": nrm(ks[14], (N_B, D_MODEL, QK_WIDTH), D_MODEL ** -0.5),
        "lambda_qk": nrm(ks[15], (N_B, 4, HEAD_DIM), 0.1),
        "subln_gain": 1.0 + nrm(ks[16], (N_B, V_DIM), 0.05),
        "w_o": nrm(ks[17], (N_B, V_WIDTH, D_MODEL), V_WIDTH ** -0.5),
        "ffn_norm": 1.0 + nrm(ks[18], (DEPTH, D_MODEL), 0.05),
        "peer_wq": nrm(ks[19], (DEPTH, D_MODEL, PEER_HEADS * PEER_QDIM), D_MODEL ** -0.5),
        "peer_subkeys": nrm(ks[20], (DEPTH, PEER_HEADS, 2, PEER_KEYS, PEER_HALF), PEER_HALF ** -0.5),
        "peer_u": nrm(ks[21], (DEPTH, PEER_EXPERTS, D_MODEL), D_MODEL ** -0.5),
        "peer_v": nrm(ks[22], (DEPTH, PEER_EXPERTS, D_MODEL), PEER_HEADS ** -0.5),
        "final_norm": 1.0 + nrm(ks[23], (D_MODEL,), 0.05),
    }


def reference(x_prompt, x_sample, state_pool, cache_k, cache_v, page_table, meta_tokens,
              pool_norm, pool_w, pool_scale, kv_norm, w_k, w_v, attn_norm, w_q, lambda_qk,
              subln_gain, w_o, ffn_norm, peer_wq, peer_subkeys, peer_u, peer_v, final_norm):
    B = x_prompt.shape[0]
    meta = jnp.broadcast_to(meta_tokens.astype(x_prompt.dtype)[None], (B, N_META, D_MODEL))
    x = jnp.concatenate([meta, x_prompt], axis=1)
    T = x.shape[1]
    pool_p = []
    for l in range(DEPTH):
        if l < N_A:
            h = rmsnorm(x, pool_norm[l])
            x = x + pool_mix(h, 0, 0, pool_w[l], pool_scale[l])
            pool_p.append(h[:, -POOL_CTX:])
        else:
            if l == N_A:
                k_p, v_p = shared_kv(x, kv_norm, w_k, w_v)
            b = l - N_A
            h = rmsnorm(x, attn_norm[b])
            q = (h @ w_q[b]).reshape(B, T, N_HEADS, 2, HEAD_DIM)
            lam = diff_lambda(lambda_qk[b], LAMBDA_INIT[b])
            o = diff_attn_prompt(q, k_p, v_p, lam)
            x = x + diff_out(o, subln_gain[b], LAMBDA_INIT[b], w_o[b])
        x = x + peer(rmsnorm(x, ffn_norm[l]), peer_wq[l], peer_subkeys[l], peer_u[l], peer_v[l])
    y_prompt = rmsnorm(x, final_norm)[:, N_META:]
    new_pool_prompt = jnp.stack(pool_p, axis=0)
    new_k_prompt = k_p.reshape(B, T, N_HEADS, 2 * HEAD_DIM)
    new_v_prompt = v_p

    DB, S = x_sample.shape[:2]
    x = x_sample
    pool_s = []
    for l in range(DEPTH):
        if l < N_A:
            h = rmsnorm(x, pool_norm[l])
            full = jnp.concatenate([state_pool[l].astype(h.dtype), h], axis=1)
            x = x + pool_mix(full, PAST_LEN - POOL_CTX, POOL_CTX, pool_w[l], pool_scale[l])
            pool_s.append(full[:, -POOL_CTX:])
        else:
            if l == N_A:
                k_s, v_s = shared_kv(x, kv_norm, w_k, w_v)
            b = l - N_A
            h = rmsnorm(x, attn_norm[b])
            q = (h @ w_q[b]).reshape(DB, S, N_HEADS, 2, HEAD_DIM)
            lam = diff_lambda(lambda_qk[b], LAMBDA_INIT[b])
            o = diff_attn_sample(q, k_s, v_s, cache_k, cache_v, page_table, lam)
            x = x + diff_out(o, subln_gain[b], LAMBDA_INIT[b], w_o[b])
        x = x + peer(rmsnorm(x, ffn_norm[l]), peer_wq[l], peer_subkeys[l], peer_u[l], peer_v[l])
    y_sample = rmsnorm(x, final_norm)
    new_pool_sample = jnp.stack(pool_s, axis=0)
    new_k_sample = k_s.reshape(DB, S, N_HEADS, 2 * HEAD_DIM)
    new_v_sample = v_s

    return (y_prompt, y_sample, new_pool_prompt, new_pool_sample,
            new_k_prompt, new_v_prompt, new_k_sample, new_v_sample)
```

```python
import functools
import math

import jax
import jax.numpy as jnp
from jax import lax
from jax.experimental import pallas as pl
from jax.experimental.pallas import tpu as pltpu

D_MODEL = 1024
BATCH = 8
SEQ = 2048
DEPTH = 2
DEC_BATCH = 32
PAST_LEN = 16384
PAGE_SIZE = 128
N_META = 16
N_A = DEPTH // 2
POOL_WINDOWS = (2, 4, 8, 16)
POOL_GROUP = D_MODEL // len(POOL_WINDOWS)
MAX_WIN = max(POOL_WINDOWS)
POOL_CTX = MAX_WIN - 1
HEAD_DIM = 64
V_DIM = 2 * HEAD_DIM
N_HEADS = D_MODEL // V_DIM
PEER_HEADS = 8
PEER_KEYS = 128
PEER_EXPERTS = PEER_KEYS * PEER_KEYS
PEER_TOPK = 16
PEER_QDIM = 256
PEER_HALF = PEER_QDIM // 2
EPS = 1e-6
LAMBDA_INIT = tuple(0.8 - 0.6 * math.exp(-0.3 * l) for l in range(N_A, DEPTH))

T_PROMPT = SEQ + N_META
N_PROMPT = BATCH * T_PROMPT
TOKEN_TILE = 512
N_TOKENS = -(-(N_PROMPT + DEC_BATCH) // TOKEN_TILE) * TOKEN_TILE
LANE = 128
EXPERT_CHUNK = 1024
POOL_T_TILE = 688
ATTN_Q_TILE = 344
PAGES_PER_STEP = 8
VMEM_LIMIT = 56 * 1024 * 1024
NEG_INF = float("-inf")

_f32 = jnp.float32
_bf16 = jnp.bfloat16


def _rms(x, g):
    return x * lax.rsqrt(jnp.mean(x * x, axis=-1, keepdims=True) + EPS) * g


def _gelu_tanh(x):
    return 0.5 * x * (1.0 + jnp.tanh(math.sqrt(2.0 / math.pi) * (x + 0.044715 * (x * x * x))))


def _dot_nt(a, b):
    return lax.dot_general(a, b, (((1,), (1,)), ((), ())), preferred_element_type=_f32)


def _diff_lambda(lam_ref, lam_init):
    lf = lam_ref[...]
    t1 = jnp.sum(lf[0:1] * lf[1:2], axis=-1, keepdims=True)
    t2 = jnp.sum(lf[2:3] * lf[3:4], axis=-1, keepdims=True)
    return jnp.exp(t1) - jnp.exp(t2) + lam_init


def _pool_prompt_kernel(x_ref, g_ref, w_ref, sc_ref, o_ref, hl_ref, hext):
    ti = pl.program_id(1)
    tt = POOL_T_TILE
    x = x_ref[0]
    h = _rms(x, g_ref[...])

    @pl.when(ti == 0)
    def _():
        hext[0:MAX_WIN, :] = jnp.zeros((MAX_WIN, D_MODEL), _f32)

    @pl.when(ti > 0)
    def _():
        hext[0:MAX_WIN, :] = hext[tt:tt + MAX_WIN, :]

    hext[MAX_WIN:, :] = h
    pos = ti * tt + lax.broadcasted_iota(jnp.int32, (tt, 1), 0)
    for g, w in enumerate(POOL_WINDOWS):
        c0 = g * POOL_GROUP
        win = hext[MAX_WIN:MAX_WIN + tt, c0:c0 + POOL_GROUP]
        for i in range(1, w):
            win = win + hext[MAX_WIN - i:MAX_WIN - i + tt, c0:c0 + POOL_GROUP]
        cnt = jnp.minimum(pos + 1, w).astype(_f32)
        d = win / cnt - h[:, c0:c0 + POOL_GROUP]
        y = jnp.dot(d.astype(_bf16), w_ref[g], preferred_element_type=_f32)
        o_ref[0, :, c0:c0 + POOL_GROUP] = x[:, c0:c0 + POOL_GROUP] + y * sc_ref[:, c0:c0 + POOL_GROUP]

    @pl.when(ti == pl.num_programs(1) - 1)
    def _():
        hl_ref[0] = h[tt - MAX_WIN:, :]


def _pool_prompt(x, g, w_bf16, sc):
    nt = T_PROMPT // POOL_T_TILE
    return pl.pallas_call(
        _pool_prompt_kernel,
        grid=(BATCH, nt),
        in_specs=[
            pl.BlockSpec((1, POOL_T_TILE, D_MODEL), lambda b, t: (b, t, 0)),
            pl.BlockSpec((1, D_MODEL), lambda b, t: (0, 0)),
            pl.BlockSpec((len(POOL_WINDOWS), POOL_GROUP, POOL_GROUP), lambda b, t: (0, 0, 0)),
            pl.BlockSpec((1, D_MODEL), lambda b, t: (0, 0)),
        ],
        out_specs=[
            pl.BlockSpec((1, POOL_T_TILE, D_MODEL), lambda b, t: (b, t, 0)),
            pl.BlockSpec((1, MAX_WIN, D_MODEL), lambda b, t: (b, 0, 0)),
        ],
        out_shape=[
            jax.ShapeDtypeStruct((BATCH, T_PROMPT, D_MODEL), _f32),
            jax.ShapeDtypeStruct((BATCH, MAX_WIN, D_MODEL), _f32),
        ],
        scratch_shapes=[pltpu.VMEM((POOL_T_TILE + MAX_WIN, D_MODEL), _f32)],
        compiler_params=pltpu.CompilerParams(
            dimension_semantics=("arbitrary", "arbitrary"), vmem_limit_bytes=VMEM_LIMIT),
        name="pool_prompt",
    )(x, g, w_bf16, sc)


def _pool_sample_kernel(x_ref, st_ref, g_ref, w_ref, sc_ref, o_ref, h_ref):
    x = x_ref[...]
    h = _rms(x, g_ref[...])
    h_ref[...] = h
    for g, w in enumerate(POOL_WINDOWS):
        c0 = g * POOL_GROUP
        win = h[:, c0:c0 + POOL_GROUP]
        for i in range(1, w):
            win = win + st_ref[:, POOL_CTX - i, c0:c0 + POOL_GROUP]
        d = win / float(w) - h[:, c0:c0 + POOL_GROUP]
        y = jnp.dot(d.astype(_bf16), w_ref[g], preferred_element_type=_f32)
        o_ref[:, c0:c0 + POOL_GROUP] = x[:, c0:c0 + POOL_GROUP] + y * sc_ref[:, c0:c0 + POOL_GROUP]


def _pool_sample(x, state, g, w_bf16, sc):
    return pl.pallas_call(
        _pool_sample_kernel,
        out_shape=[jax.ShapeDtypeStruct((DEC_BATCH, D_MODEL), _f32),
                   jax.ShapeDtypeStruct((DEC_BATCH, D_MODEL), _f32)],
        compiler_params=pltpu.CompilerParams(vmem_limit_bytes=VMEM_LIMIT),
        name="pool_sample",
    )(x, state, g, w_bf16, sc)


def _top16_rows(s):
    rows = []
    for _ in range(PEER_TOPK):
        m = jnp.max(s, axis=0, keepdims=True)
        rows.append(m)
        s = jnp.where(s == m, NEG_INF, s)
    return rows


def _peer_route_unit(u, q_scr, sk_ref, xp_scr, yp_scr, e1_scr, e2_scr, tau_scr, n_chunks):
    h = u // n_chunks
    ch = u % n_chunks
    tok = pl.ds(pl.multiple_of(ch * LANE, LANE), LANE)
    halves = []
    for c in range(2):
        col = pl.ds(pl.multiple_of(h * PEER_QDIM + c * PEER_HALF, LANE), PEER_HALF)
        q = q_scr[tok, col].astype(_bf16)
        s = _dot_nt(sk_ref[h, c], q)
        rows = _top16_rows(s)
        keep = s >= rows[PEER_TOPK - 1]
        halves.append((s, rows, keep))
    (s1, a_rows, keep1), (s2, b_rows, keep2) = halves
    a8 = jnp.concatenate(a_rows[8:16], axis=0)
    b8 = jnp.concatenate(b_rows[0:8], axis=0)
    b16 = jnp.concatenate(b_rows, axis=0)
    cands = [a_rows[0] + b16]
    cands += [a_rows[i] + b8 for i in range(1, 8)]
    cands += [a8 + b_rows[0]]
    top = _top16_rows(jnp.concatenate(cands, axis=0))
    tau = top[PEER_TOPK - 1]
    z = jnp.ones_like(tau)
    for k in range(1, PEER_TOPK):
        z = z + jnp.exp(top[k] - top[0])
    xp_scr[h, :, tok] = jnp.where(keep1, s1, NEG_INF)
    yp_scr[h, :, tok] = jnp.where(keep2, s2, NEG_INF)
    e1_scr[h, :, tok] = jnp.exp(s1 - a_rows[0]) / z
    e2_scr[h, :, tok] = jnp.exp(s2 - b_rows[0])
    tau_scr[h, :, tok] = tau


def _peer_kernel(x_ref, g_ref, wq_ref, sk_ref, u_ref, vt_ref, fn_ref, o_ref,
                 hb_scr, q_scr, xp_scr, yp_scr, e1_scr, e2_scr, tau_scr, g_scr, p_scr, acc_scr,
                 *, final_norm):
    c = pl.program_id(1)
    tn = TOKEN_TILE
    n_chunks = tn // LANE
    blocks = EXPERT_CHUNK // PEER_KEYS

    @pl.when(c == 0)
    def _():
        hb = _rms(x_ref[...], g_ref[...]).astype(_bf16)
        hb_scr[...] = hb
        q_scr[...] = jnp.dot(hb, wq_ref[...], preferred_element_type=_f32)
        acc_scr[...] = jnp.zeros_like(acc_scr)

        def unit(u, carry):
            _peer_route_unit(u, q_scr, sk_ref, xp_scr, yp_scr, e1_scr, e2_scr, tau_scr, n_chunks)
            return carry

        lax.fori_loop(0, PEER_HEADS * n_chunks, unit, 0)

    g_scr[...] = _dot_nt(u_ref[...], hb_scr[...])

    a_rows = pl.ds(pl.multiple_of(c * blocks, blocks), blocks)

    def token_chunk(ch, carry):
        tok = pl.ds(pl.multiple_of(ch * LANE, LANE), LANE)
        xg = [xp_scr[h, a_rows, tok] for h in range(PEER_HEADS)]
        eg = [e1_scr[h, a_rows, tok] for h in range(PEER_HEADS)]
        for ab in range(blocks):
            rows = slice(ab * PEER_KEYS, (ab + 1) * PEER_KEYS)
            act = _gelu_tanh(g_scr[rows, tok])
            w = jnp.zeros((PEER_KEYS, LANE), _f32)
            for h in range(PEER_HEADS):
                cand = xg[h][ab:ab + 1] + yp_scr[h, :, tok]
                gate = eg[h][ab:ab + 1] * e2_scr[h, :, tok]
                w = w + jnp.where(cand >= tau_scr[h, :, tok], gate, 0.0)
            p_scr[rows, tok] = (w * act).astype(_bf16)
        return carry

    lax.fori_loop(0, n_chunks, token_chunk, 0)
    acc_scr[...] += jnp.dot(vt_ref[...], p_scr[...], preferred_element_type=_f32)

    @pl.when(c == pl.num_programs(1) - 1)
    def _():
        y = x_ref[...] + acc_scr[...].T
        if final_norm:
            y = _rms(y, fn_ref[...])
        o_ref[...] = y


def _peer(x, g, wq_bf16, sk_bf16, u_bf16, vt_bf16, fn, final_norm):
    tn = TOKEN_TILE
    n_ec = PEER_EXPERTS // EXPERT_CHUNK
    return pl.pallas_call(
        functools.partial(_peer_kernel, final_norm=final_norm),
        grid=(N_TOKENS // tn, n_ec),
        in_specs=[
            pl.BlockSpec((tn, D_MODEL), lambda i, c: (i, 0)),
            pl.BlockSpec((1, D_MODEL), lambda i, c: (0, 0)),
            pl.BlockSpec((D_MODEL, PEER_HEADS * PEER_QDIM), lambda i, c: (0, 0)),
            pl.BlockSpec((PEER_HEADS, 2, PEER_KEYS, PEER_HALF), lambda i, c: (0, 0, 0, 0)),
            pl.BlockSpec((EXPERT_CHUNK, D_MODEL), lambda i, c: (c, 0)),
            pl.BlockSpec((D_MODEL, EXPERT_CHUNK), lambda i, c: (0, c)),
            pl.BlockSpec((1, D_MODEL), lambda i, c: (0, 0)),
        ],
        out_specs=pl.BlockSpec((tn, D_MODEL), lambda i, c: (i, 0)),
        out_shape=jax.ShapeDtypeStruct((N_TOKENS, D_MODEL), _f32),
        scratch_shapes=[
            pltpu.VMEM((tn, D_MODEL), _bf16),
            pltpu.VMEM((tn, PEER_HEADS * PEER_QDIM), _f32),
            pltpu.VMEM((PEER_HEADS, PEER_KEYS, tn), _f32),
            pltpu.VMEM((PEER_HEADS, PEER_KEYS, tn), _f32),
            pltpu.VMEM((PEER_HEADS, PEER_KEYS, tn), _f32),
            pltpu.VMEM((PEER_HEADS, PEER_KEYS, tn), _f32),
            pltpu.VMEM((PEER_HEADS, 1, tn), _f32),
            pltpu.VMEM((EXPERT_CHUNK, tn), _f32),
            pltpu.VMEM((EXPERT_CHUNK, tn), _bf16),
            pltpu.VMEM((D_MODEL, tn), _f32),
        ],
        compiler_params=pltpu.CompilerParams(
            dimension_semantics=("arbitrary", "arbitrary"), vmem_limit_bytes=VMEM_LIMIT),
        name="peer_final" if final_norm else "peer",
    )(x, g, wq_bf16, sk_bf16, u_bf16, vt_bf16, fn)


def _qkv_kernel(x_ref, kvg_ref, ag_ref, wk_ref, wv_ref, wq_ref, k_ref, v_ref, kb_ref, vb_ref, qb_ref):
    x = x_ref[...]
    hk = _rms(x, kvg_ref[...]).astype(_bf16)
    k = jnp.dot(hk, wk_ref[...], preferred_element_type=_f32)
    v = jnp.dot(hk, wv_ref[...], preferred_element_type=_f32)
    k_ref[...] = k
    v_ref[...] = v
    kb_ref[...] = k.astype(_bf16)
    vb_ref[...] = v.astype(_bf16)
    ha = _rms(x, ag_ref[...]).astype(_bf16)
    q = jnp.dot(ha, wq_ref[...], preferred_element_type=_f32)
    qb_ref[...] = (q * (HEAD_DIM ** -0.5)).astype(_bf16)


def _qkv(x, kvg, ag, wk, wv, wq):
    tn = TOKEN_TILE
    row = pl.BlockSpec((tn, D_MODEL), lambda i: (i, 0))
    vec = pl.BlockSpec((1, D_MODEL), lambda i: (0, 0))
    mat = pl.BlockSpec((D_MODEL, D_MODEL), lambda i: (0, 0))
    return pl.pallas_call(
        _qkv_kernel,
        grid=(N_TOKENS // tn,),
        in_specs=[row, vec, vec, mat, mat, mat],
        out_specs=[row, row, row, row, row],
        out_shape=[jax.ShapeDtypeStruct((N_TOKENS, D_MODEL), _f32),
                   jax.ShapeDtypeStruct((N_TOKENS, D_MODEL), _f32),
                   jax.ShapeDtypeStruct((N_TOKENS, D_MODEL), _bf16),
                   jax.ShapeDtypeStruct((N_TOKENS, D_MODEL), _bf16),
                   jax.ShapeDtypeStruct((N_TOKENS, D_MODEL), _bf16)],
        compiler_params=pltpu.CompilerParams(
            dimension_semantics=("arbitrary",), vmem_limit_bytes=VMEM_LIMIT),
        name="qkv_proj",
    )(x, kvg, ag, wk, wv, wq)


def _attn_prompt_kernel(q_ref, k_ref, v_ref, lam_ref, sg_ref, o_ref, *, lam_init):
    hd = pl.program_id(1)
    slope = jnp.exp2(-(hd + 1).astype(_f32))
    lam = _diff_lambda(lam_ref, lam_init)
    tq = ATTN_Q_TILE
    lane = lax.broadcasted_iota(jnp.int32, (1, V_DIM), 1)
    first_half = lane < HEAD_DIM
    for i in range(T_PROMPT // tq):
        kv = (i + 1) * tq
        q = q_ref[i * tq:(i + 1) * tq, :]
        k = k_ref[0:kv, :]
        v = v_ref[0:kv, :]
        qpos = i * tq + lax.broadcasted_iota(jnp.int32, (tq, 1), 0)
        kpos = lax.broadcasted_iota(jnp.int32, (1, kv), 1)
        dist = (qpos - kpos).astype(_f32)
        visible = dist >= 0.0
        bias = slope * dist
        probs = []
        for c in range(2):
            qc = jnp.where(first_half if c == 0 else jnp.logical_not(first_half), q, jnp.zeros_like(q))
            s = _dot_nt(qc, k) - bias
            s = jnp.where(visible, s, NEG_INF)
            p = jnp.exp(s - jnp.max(s, axis=-1, keepdims=True))
            probs.append(p / jnp.sum(p, axis=-1, keepdims=True))
        a = probs[0] - lam * probs[1]
        o = jnp.dot(a.astype(_bf16), v, preferred_element_type=_f32)
        o = _rms(o, sg_ref[...]) * (1.0 - lam_init)
        o_ref[i * tq:(i + 1) * tq, :] = o.astype(_bf16)


def _attn_prompt(qb, kb, vb, lam_qk, sg, lam_init):
    blk = pl.BlockSpec((T_PROMPT, V_DIM), lambda b, h: (b, h))
    return pl.pallas_call(
        functools.partial(_attn_prompt_kernel, lam_init=lam_init),
        grid=(BATCH, N_HEADS),
        in_specs=[blk, blk, blk,
                  pl.BlockSpec((4, HEAD_DIM), lambda b, h: (0, 0)),
                  pl.BlockSpec((1, V_DIM), lambda b, h: (0, 0))],
        out_specs=blk,
        out_shape=jax.ShapeDtypeStruct((N_PROMPT, D_MODEL), _bf16),
        compiler_params=pltpu.CompilerParams(
            dimension_semantics=("arbitrary", "arbitrary"), vmem_limit_bytes=VMEM_LIMIT),
        name="attn_prompt",
    )(qb, kb, vb, lam_qk, sg)


def _attn_sample_kernel(pt_ref, q_ref, kn_ref, vn_ref, lam_ref, sg_ref, *rest, lam_init):
    npg = PAGES_PER_STEP
    k_refs = rest[:npg]
    v_refs = rest[npg:2 * npg]
    o_ref, qm_scr, m_scr, l_scr, acc_scr = rest[2 * npg:]
    j = pl.program_id(1)
    rows = 2 * N_HEADS

    @pl.when(j == 0)
    def _():
        r = lax.broadcasted_iota(jnp.int32, (rows, D_MODEL), 0)
        col = lax.broadcasted_iota(jnp.int32, (rows, D_MODEL), 1)
        own = (col // V_DIM == r % N_HEADS) & ((col // HEAD_DIM) % 2 == r // N_HEADS)
        qm = jnp.where(own, q_ref[0], 0.0)
        qm_scr[...] = qm.astype(_bf16)
        kn = kn_ref[0].astype(_bf16).astype(_f32)
        vn = vn_ref[0].astype(_bf16).astype(_f32)
        m_scr[...] = jnp.sum(qm * kn, axis=-1, keepdims=True)
        l_scr[...] = jnp.ones_like(l_scr)
        acc_scr[...] = jnp.broadcast_to(vn, (rows, D_MODEL))

    qm = qm_scr[...]
    s = jnp.concatenate([_dot_nt(qm, k_refs[i][0].astype(_bf16)) for i in range(npg)], axis=1)
    width = npg * PAGE_SIZE
    kpos = j * width + lax.broadcasted_iota(jnp.int32, (1, width), 1)
    dist = (PAST_LEN - kpos).astype(_f32)
    head = lax.broadcasted_iota(jnp.int32, (rows, 1), 0) % N_HEADS
    slope = jnp.exp2(-(head + 1).astype(_f32))
    s = s - slope * dist
    m_old = m_scr[...]
    m_new = jnp.maximum(m_old, jnp.max(s, axis=-1, keepdims=True))
    corr = jnp.exp(m_old - m_new)
    p = jnp.exp(s - m_new)
    l_scr[...] = l_scr[...] * corr + jnp.sum(p, axis=-1, keepdims=True)
    pb = p.astype(_bf16)
    pv = jnp.zeros((rows, D_MODEL), _f32)
    for i in range(npg):
        pv = pv + jnp.dot(pb[:, i * PAGE_SIZE:(i + 1) * PAGE_SIZE], v_refs[i][0].astype(_bf16),
                          preferred_element_type=_f32)
    acc_scr[...] = acc_scr[...] * corr + pv
    m_scr[...] = m_new

    @pl.when(j == pl.num_programs(1) - 1)
    def _():
        lam = _diff_lambda(lam_ref, lam_init)
        o = acc_scr[...] / l_scr[...]
        diff = o[0:N_HEADS] - lam * o[N_HEADS:rows]
        r = lax.broadcasted_iota(jnp.int32, (N_HEADS, D_MODEL), 0)
        col = lax.broadcasted_iota(jnp.int32, (N_HEADS, D_MODEL), 1)
        dm = jnp.where(col // V_DIM == r, diff, 0.0)
        ms = jnp.sum(dm * dm, axis=-1, keepdims=True) / float(V_DIM)
        nrm = dm * lax.rsqrt(ms + EPS)
        o_ref[0] = jnp.sum(nrm, axis=0, keepdims=True) * sg_ref[...] * (1.0 - lam_init)


def _attn_sample(page_table, q_s, k_s, v_s, lam_qk, sg_tiled, cache_k, cache_v, lam_init):
    npg = PAGES_PER_STEP
    n_pages = page_table.shape[1]
    row = pl.BlockSpec((1, 1, D_MODEL), lambda b, j, pt: (b, 0, 0))

    def page_spec(i):
        return pl.BlockSpec((1, PAGE_SIZE, D_MODEL), lambda b, j, pt: (pt[b, j * npg + i], 0, 0))

    grid_spec = pltpu.PrefetchScalarGridSpec(
        num_scalar_prefetch=1,
        grid=(DEC_BATCH, n_pages // npg),
        in_specs=[row, row, row,
                  pl.BlockSpec((4, HEAD_DIM), lambda b, j, pt: (0, 0)),
                  pl.BlockSpec((1, D_MODEL), lambda b, j, pt: (0, 0))]
                 + [page_spec(i) for i in range(npg)] + [page_spec(i) for i in range(npg)],
        out_specs=row,
        scratch_shapes=[pltpu.VMEM((2 * N_HEADS, D_MODEL), _bf16),
                        pltpu.VMEM((2 * N_HEADS, 1), _f32),
                        pltpu.VMEM((2 * N_HEADS, 1), _f32),
                        pltpu.VMEM((2 * N_HEADS, D_MODEL), _f32)],
    )
    return pl.pallas_call(
        functools.partial(_attn_sample_kernel, lam_init=lam_init),
        grid_spec=grid_spec,
        out_shape=jax.ShapeDtypeStruct((DEC_BATCH, 1, D_MODEL), _f32),
        compiler_params=pltpu.CompilerParams(
            dimension_semantics=("arbitrary", "arbitrary"), vmem_limit_bytes=VMEM_LIMIT),
        name="attn_sample",
    )(page_table, q_s, k_s, v_s, lam_qk, sg_tiled, *([cache_k] * npg), *([cache_v] * npg))


def _oproj_kernel(x_ref, o_ref, w_ref, y_ref):
    y_ref[...] = x_ref[...] + jnp.dot(o_ref[...], w_ref[...], preferred_element_type=_f32)


def _oproj(x, o_bf16, w_bf16):
    tn = TOKEN_TILE
    row = pl.BlockSpec((tn, D_MODEL), lambda i: (i, 0))
    return pl.pallas_call(
        _oproj_kernel,
        grid=(N_TOKENS // tn,),
        in_specs=[row, row, pl.BlockSpec((D_MODEL, D_MODEL), lambda i: (0, 0))],
        out_specs=row,
        out_shape=jax.ShapeDtypeStruct((N_TOKENS, D_MODEL), _f32),
        compiler_params=pltpu.CompilerParams(
            dimension_semantics=("arbitrary",), vmem_limit_bytes=VMEM_LIMIT),
        name="attn_out_proj",
    )(x, o_bf16, w_bf16)


def kernel(x_prompt, x_sample, state_pool, cache_k, cache_v, page_table, meta_tokens, pool_norm, pool_w, pool_scale, kv_norm, w_k, w_v, attn_norm, w_q, lambda_qk, subln_gain, w_o, ffn_norm, peer_wq, peer_subkeys, peer_u, peer_v, final_norm):
    assert DEPTH == 2 and N_A == 1
    vec = lambda a: a.reshape(1, -1).astype(_f32)
    pad_rows = N_TOKENS - N_PROMPT - DEC_BATCH

    def peer_layer(x, l, final):
        return _peer(x, vec(ffn_norm[l]), peer_wq[l].astype(_bf16), peer_subkeys[l].astype(_bf16),
                     peer_u[l].astype(_bf16), peer_v[l].T.astype(_bf16), vec(final_norm), final)

    meta = jnp.broadcast_to(meta_tokens[None], (BATCH, N_META, D_MODEL))
    xp = jnp.concatenate([meta, x_prompt], axis=1)
    pw = pool_w[0].astype(_bf16)
    xp1, h_last = _pool_prompt(xp, vec(pool_norm[0]), pw, vec(pool_scale[0]))
    xs1, h_s = _pool_sample(x_sample.reshape(DEC_BATCH, D_MODEL), state_pool[0],
                            vec(pool_norm[0]), pw, vec(pool_scale[0]))
    new_pool_prompt = h_last[None, :, 1:]
    new_pool_sample = jnp.concatenate([state_pool[0][:, 1:], h_s[:, None]], axis=1)[None]

    x = jnp.concatenate([xp1.reshape(N_PROMPT, D_MODEL), xs1,
                         jnp.zeros((pad_rows, D_MODEL), _f32)], axis=0)
    x = peer_layer(x, 0, False)

    k, v, kb, vb, qb = _qkv(x, vec(kv_norm), vec(attn_norm[0]), w_k.astype(_bf16), w_v.astype(_bf16),
                            w_q[0].astype(_bf16))
    lam_init = LAMBDA_INIT[0]
    sg = vec(subln_gain[0])
    o_p = _attn_prompt(qb, kb, vb, lambda_qk[0], sg, lam_init)
    rows_s = slice(N_PROMPT, N_PROMPT + DEC_BATCH)
    k_s = k[rows_s].reshape(DEC_BATCH, 1, D_MODEL)
    v_s = v[rows_s].reshape(DEC_BATCH, 1, D_MODEL)
    q_s = qb[rows_s].astype(_f32).reshape(DEC_BATCH, 1, D_MODEL)
    n_pool_pages = cache_k.shape[0]
    o_s = _attn_sample(page_table, q_s, k_s, v_s, lambda_qk[0], jnp.tile(sg, (1, N_HEADS)),
                       cache_k.reshape(n_pool_pages, PAGE_SIZE, D_MODEL),
                       cache_v.reshape(n_pool_pages, PAGE_SIZE, D_MODEL), lam_init)
    o_all = jnp.concatenate([o_p, o_s.reshape(DEC_BATCH, D_MODEL).astype(_bf16),
                             jnp.zeros((pad_rows, D_MODEL), _bf16)], axis=0)
    x = _oproj(x, o_all, w_o[0].astype(_bf16))
    y = peer_layer(x, 1, True)

    y_prompt = y[:N_PROMPT].reshape(BATCH, T_PROMPT, D_MODEL)[:, N_META:]
    y_sample = y[rows_s].reshape(DEC_BATCH, 1, D_MODEL)
    new_k_prompt = k[:N_PROMPT].reshape(BATCH, T_PROMPT, N_HEADS, 2 * HEAD_DIM)
    new_v_prompt = v[:N_PROMPT].reshape(BATCH, T_PROMPT, N_HEADS, V_DIM)
    new_k_sample = k_s.reshape(DEC_BATCH, 1, N_HEADS, 2 * HEAD_DIM)
    new_v_sample = v_s.reshape(DEC_BATCH, 1, N_HEADS, V_DIM)
    return (y_prompt, y_sample, new_pool_prompt, new_pool_sample,
            new_k_prompt, new_v_prompt, new_k_sample, new_v_sample)
```

```python
import functools
import math

import jax
import jax.numpy as jnp
from jax import lax
from jax.experimental import pallas as pl
from jax.experimental.pallas import tpu as pltpu

D_MODEL = 1024
BATCH = 8
SEQ = 2048
DEPTH = 2
DEC_BATCH = 32
PAST_LEN = 16384
PAGE_SIZE = 128
N_META = 16
N_A = DEPTH // 2
POOL_WINDOWS = (2, 4, 8, 16)
POOL_GROUP = D_MODEL // len(POOL_WINDOWS)
MAX_WIN = max(POOL_WINDOWS)
POOL_CTX = MAX_WIN - 1
HEAD_DIM = 64
V_DIM = 2 * HEAD_DIM
N_HEADS = D_MODEL // V_DIM
PEER_HEADS = 8
PEER_KEYS = 128
PEER_EXPERTS = PEER_KEYS * PEER_KEYS
PEER_TOPK = 16
PEER_QDIM = 256
PEER_HALF = PEER_QDIM // 2
EPS = 1e-6
LAMBDA_INIT = tuple(0.8 - 0.6 * math.exp(-0.3 * l) for l in range(N_A, DEPTH))

T_PROMPT = SEQ + N_META
N_PROMPT = BATCH * T_PROMPT
TOKEN_TILE = 512
N_TOKENS = -(-(N_PROMPT + DEC_BATCH) // TOKEN_TILE) * TOKEN_TILE
LANE = 128
EXPERT_CHUNK = 1024
POOL_T_TILE = 688
ATTN_Q_TILE = 344
PAGES_PER_STEP = 8
VMEM_LIMIT = 56 * 1024 * 1024
NEG_INF = float("-inf")

_f32 = jnp.float32
_bf16 = jnp.bfloat16


def _rms(x, g):
    return x * lax.rsqrt(jnp.mean(x * x, axis=-1, keepdims=True) + EPS) * g


def _gelu_tanh(x):
    return 0.5 * x * (1.0 + jnp.tanh(math.sqrt(2.0 / math.pi) * (x + 0.044715 * (x * x * x))))


def _dot_nt(a, b):
    return lax.dot_general(a, b, (((1,), (1,)), ((), ())), preferred_element_type=_f32)


def _diff_lambda(lam_ref, lam_init):
    lf = lam_ref[...]
    t1 = jnp.sum(lf[0:1] * lf[1:2], axis=-1, keepdims=True)
    t2 = jnp.sum(lf[2:3] * lf[3:4], axis=-1, keepdims=True)
    return jnp.exp(t1) - jnp.exp(t2) + lam_init


def _pool_prompt_kernel(x_ref, g_ref, w_ref, sc_ref, o_ref, hl_ref, hext):
    ti = pl.program_id(1)
    tt = POOL_T_TILE
    x = x_ref[0]
    h = _rms(x, g_ref[...])

    @pl.when(ti == 0)
    def _():
        hext[0:MAX_WIN, :] = jnp.zeros((MAX_WIN, D_MODEL), _f32)

    @pl.when(ti > 0)
    def _():
        hext[0:MAX_WIN, :] = hext[tt:tt + MAX_WIN, :]

    hext[MAX_WIN:, :] = h
    pos = ti * tt + lax.broadcasted_iota(jnp.int32, (tt, 1), 0)
    for g, w in enumerate(POOL_WINDOWS):
        c0 = g * POOL_GROUP
        win = hext[MAX_WIN:MAX_WIN + tt, c0:c0 + POOL_GROUP]
        for i in range(1, w):
            win = win + hext[MAX_WIN - i:MAX_WIN - i + tt, c0:c0 + POOL_GROUP]
        cnt = jnp.minimum(pos + 1, w).astype(_f32)
        d = win / cnt - h[:, c0:c0 + POOL_GROUP]
        y = jnp.dot(d.astype(_bf16), w_ref[g], preferred_element_type=_f32)
        o_ref[0, :, c0:c0 + POOL_GROUP] = x[:, c0:c0 + POOL_GROUP] + y * sc_ref[:, c0:c0 + POOL_GROUP]

    @pl.when(ti == pl.num_programs(1) - 1)
    def _():
        hl_ref[0] = h[tt - MAX_WIN:, :]


def _pool_prompt(x, g, w_bf16, sc):
    nt = T_PROMPT // POOL_T_TILE
    return pl.pallas_call(
        _pool_prompt_kernel,
        grid=(BATCH, nt),
        in_specs=[
            pl.BlockSpec((1, POOL_T_TILE, D_MODEL), lambda b, t: (b, t, 0)),
            pl.BlockSpec((1, D_MODEL), lambda b, t: (0, 0)),
            pl.BlockSpec((len(POOL_WINDOWS), POOL_GROUP, POOL_GROUP), lambda b, t: (0, 0, 0)),
            pl.BlockSpec((1, D_MODEL), lambda b, t: (0, 0)),
        ],
        out_specs=[
            pl.BlockSpec((1, POOL_T_TILE, D_MODEL), lambda b, t: (b, t, 0)),
            pl.BlockSpec((1, MAX_WIN, D_MODEL), lambda b, t: (b, 0, 0)),
        ],
        out_shape=[
            jax.ShapeDtypeStruct((BATCH, T_PROMPT, D_MODEL), _f32),
            jax.ShapeDtypeStruct((BATCH, MAX_WIN, D_MODEL), _f32),
        ],
        scratch_shapes=[pltpu.VMEM((POOL_T_TILE + MAX_WIN, D_MODEL), _f32)],
        compiler_params=pltpu.CompilerParams(
            dimension_semantics=("arbitrary", "arbitrary"), vmem_limit_bytes=VMEM_LIMIT),
        name="pool_prompt",
    )(x, g, w_bf16, sc)


def _pool_sample_kernel(x_ref, st_ref, g_ref, w_ref, sc_ref, o_ref, h_ref):
    x = x_ref[...]
    h = _rms(x, g_ref[...])
    h_ref[...] = h
    for g, w in enumerate(POOL_WINDOWS):
        c0 = g * POOL_GROUP
        win = h[:, c0:c0 + POOL_GROUP]
        for i in range(1, w):
            win = win + st_ref[:, POOL_CTX - i, c0:c0 + POOL_GROUP]
        d = win / float(w) - h[:, c0:c0 + POOL_GROUP]
        y = jnp.dot(d.astype(_bf16), w_ref[g], preferred_element_type=_f32)
        o_ref[:, c0:c0 + POOL_GROUP] = x[:, c0:c0 + POOL_GROUP] + y * sc_ref[:, c0:c0 + POOL_GROUP]


def _pool_sample(x, state, g, w_bf16, sc):
    return pl.pallas_call(
        _pool_sample_kernel,
        out_shape=[jax.ShapeDtypeStruct((DEC_BATCH, D_MODEL), _f32),
                   jax.ShapeDtypeStruct((DEC_BATCH, D_MODEL), _f32)],
        compiler_params=pltpu.CompilerParams(vmem_limit_bytes=VMEM_LIMIT),
        name="pool_sample",
    )(x, state, g, w_bf16, sc)


def _top16_ranked(s):
    rows = []
    rank = jnp.full(s.shape, float(PEER_TOPK), _f32)
    for i in range(PEER_TOPK):
        m = jnp.max(s, axis=0, keepdims=True)
        rows.append(m)
        hit = s == m
        rank = jnp.where(hit, float(i), rank)
        s = jnp.where(hit, NEG_INF, s)
    return rows, rank


def _peer_route_unit(u, qt_scr, sk_ref, jf_scr, e1_scr, r2_scr, e2_scr, n_chunks):
    h = u // n_chunks
    ch = u % n_chunks
    tok = pl.ds(pl.multiple_of(ch * LANE, LANE), LANE)
    halves = []
    for c in range(2):
        d0 = pl.multiple_of((h * 2 + c) * PEER_HALF, PEER_HALF)
        slot = d0 // EXPERT_CHUNK
        q = qt_scr[slot, pl.ds(pl.multiple_of(d0 % EXPERT_CHUNK, PEER_HALF), PEER_HALF), tok]
        s = jnp.dot(sk_ref[h, c], q.astype(_bf16), preferred_element_type=_f32)
        halves.append((s,) + _top16_ranked(s))
    (s1, a_rows, r1), (s2, b_rows, r2) = halves
    a8 = jnp.concatenate(a_rows[8:16], axis=0)
    b8 = jnp.concatenate(b_rows[0:8], axis=0)
    b16 = jnp.concatenate(b_rows, axis=0)
    cands = [a_rows[0] + b16]
    cands += [a_rows[i] + b8 for i in range(1, 8)]
    cands += [a8 + b_rows[0]]
    top, _ = _top16_ranked(jnp.concatenate(cands, axis=0))
    tau = top[PEER_TOPK - 1]
    z = jnp.ones_like(tau)
    for k in range(1, PEER_TOPK):
        z = z + jnp.exp(top[k] - top[0])
    counts = [jnp.sum(jnp.where(cands[i] >= tau, 1.0, 0.0), axis=0, keepdims=True) for i in range(8)]
    tail = jnp.where(cands[8] >= tau, 1.0, 0.0)
    counts += [tail[i:i + 1] for i in range(8)]
    jcount = jnp.zeros(s1.shape, _f32)
    for i in range(PEER_TOPK):
        jcount = jnp.where(r1 == float(i), counts[i], jcount)
    jf_scr[h, :, tok] = jcount
    e1_scr[h, :, tok] = jnp.exp(s1 - a_rows[0]) / z
    r2_scr[h, :, tok] = r2.astype(_bf16)
    e2_scr[h, :, tok] = jnp.exp(s2 - b_rows[0]).astype(_bf16)


def _peer_gate_chunk(chunk, g_scr, p_scr, slot, jf_scr, e1_scr, r2_scr, e2_scr):
    blocks = EXPERT_CHUNK // PEER_KEYS
    a_rows = pl.ds(pl.multiple_of(chunk * blocks, blocks), blocks)
    for ch in range(TOKEN_TILE // LANE):
        tok = slice(ch * LANE, (ch + 1) * LANE)
        jg = [jf_scr[h, a_rows, tok] for h in range(PEER_HEADS)]
        eg = [e1_scr[h, a_rows, tok] for h in range(PEER_HEADS)]
        for ab in range(blocks):
            rows = slice(ab * PEER_KEYS, (ab + 1) * PEER_KEYS)
            act = _gelu_tanh(g_scr[slot, rows, tok]).astype(_bf16)
            w = jnp.zeros((PEER_KEYS, LANE), _bf16)
            for h in range(PEER_HEADS):
                jrow = jnp.broadcast_to(jg[h][ab:ab + 1], (PEER_KEYS, LANE)).astype(_bf16)
                erow = jnp.broadcast_to(eg[h][ab:ab + 1], (PEER_KEYS, LANE)).astype(_bf16)
                gate = erow * e2_scr[h, :, tok]
                w = w + jnp.where(r2_scr[h, :, tok] < jrow, gate, jnp.zeros_like(gate))
            p_scr[slot, rows, tok] = w * act


def _peer_kernel(x_ref, g_ref, wqt_ref, sk_ref, u0_ref, ua_ref, ub_ref, vta_ref, vtb_ref, vtl_ref, fn_ref, o_ref,
                 hb_scr, jf_scr, e1_scr, r2_scr, e2_scr, g_scr, p_scr, acc_scr, *, final_norm):
    c = pl.program_id(1)
    n_chunks = TOKEN_TILE // LANE

    @pl.when(c == 0)
    def _():
        hb = _rms(x_ref[...], g_ref[...]).astype(_bf16)
        hb_scr[...] = hb
        qt = _dot_nt(wqt_ref[...], hb)
        g_scr[0] = qt[0:EXPERT_CHUNK]
        g_scr[1] = qt[EXPERT_CHUNK:2 * EXPERT_CHUNK]

        def unit(u, carry):
            _peer_route_unit(u, g_scr, sk_ref, jf_scr, e1_scr, r2_scr, e2_scr, n_chunks)
            return carry

        lax.fori_loop(0, PEER_HEADS * n_chunks, unit, 0)
        g_scr[0] = _dot_nt(u0_ref[...], hb)
        p_scr[1] = jnp.zeros(p_scr.shape[1:], _bf16)
        acc_scr[...] = jnp.zeros_like(acc_scr)

    hb = hb_scr[...]
    g_scr[1] = _dot_nt(ua_ref[...], hb)
    _peer_gate_chunk(2 * c, g_scr, p_scr, 0, jf_scr, e1_scr, r2_scr, e2_scr)
    out_a = jnp.dot(vta_ref[...], p_scr[1], preferred_element_type=_f32)
    g_scr[0] = _dot_nt(ub_ref[...], hb)
    _peer_gate_chunk(2 * c + 1, g_scr, p_scr, 1, jf_scr, e1_scr, r2_scr, e2_scr)
    acc_scr[...] = acc_scr[...] + (out_a + jnp.dot(vtb_ref[...], p_scr[0], preferred_element_type=_f32))

    @pl.when(c == pl.num_programs(1) - 1)
    def _():
        acc = acc_scr[...] + jnp.dot(vtl_ref[...], p_scr[1], preferred_element_type=_f32)
        y = x_ref[...] + acc.T
        if final_norm:
            y = _rms(y, fn_ref[...])
        o_ref[...] = y


def _peer(x, g, wqt_bf16, sk_bf16, u_bf16, vt_bf16, fn, final_norm):
    tn = TOKEN_TILE
    n_ec = PEER_EXPERTS // EXPERT_CHUNK
    assert n_ec % 2 == 0 and EXPERT_CHUNK // PEER_KEYS == 8 and PEER_HEADS * PEER_QDIM == 2 * EXPERT_CHUNK
    last = n_ec - 1
    u_spec = lambda f: pl.BlockSpec((EXPERT_CHUNK, D_MODEL), lambda i, c: (f(c), 0))
    vt_spec = lambda f: pl.BlockSpec((D_MODEL, EXPERT_CHUNK), lambda i, c: (0, f(c)))
    return pl.pallas_call(
        functools.partial(_peer_kernel, final_norm=final_norm),
        grid=(N_TOKENS // tn, n_ec // 2),
        in_specs=[
            pl.BlockSpec((tn, D_MODEL), lambda i, c: (i, 0)),
            pl.BlockSpec((1, D_MODEL), lambda i, c: (0, 0)),
            pl.BlockSpec((PEER_HEADS * PEER_QDIM, D_MODEL), lambda i, c: (0, 0)),
            pl.BlockSpec((PEER_HEADS, 2, PEER_KEYS, PEER_HALF), lambda i, c: (0, 0, 0, 0)),
            u_spec(lambda c: 0),
            u_spec(lambda c: 2 * c + 1),
            u_spec(lambda c: jnp.minimum(2 * c + 2, last)),
            vt_spec(lambda c: jnp.maximum(2 * c - 1, 0)),
            vt_spec(lambda c: 2 * c),
            vt_spec(lambda c: last),
            pl.BlockSpec((1, D_MODEL), lambda i, c: (0, 0)),
        ],
        out_specs=pl.BlockSpec((tn, D_MODEL), lambda i, c: (i, 0)),
        out_shape=jax.ShapeDtypeStruct((N_TOKENS, D_MODEL), _f32),
        scratch_shapes=[
            pltpu.VMEM((tn, D_MODEL), _bf16),
            pltpu.VMEM((PEER_HEADS, PEER_KEYS, tn), _f32),
            pltpu.VMEM((PEER_HEADS, PEER_KEYS, tn), _f32),
            pltpu.VMEM((PEER_HEADS, PEER_KEYS, tn), _bf16),
            pltpu.VMEM((PEER_HEADS, PEER_KEYS, tn), _bf16),
            pltpu.VMEM((2, EXPERT_CHUNK, tn), _f32),
            pltpu.VMEM((2, EXPERT_CHUNK, tn), _bf16),
            pltpu.VMEM((D_MODEL, tn), _f32),
        ],
        compiler_params=pltpu.CompilerParams(
            dimension_semantics=("arbitrary", "arbitrary"), vmem_limit_bytes=VMEM_LIMIT),
        name="peer_final" if final_norm else "peer",
    )(x, g, wqt_bf16, sk_bf16, u_bf16, u_bf16, u_bf16, vt_bf16, vt_bf16, vt_bf16, fn)


def _qkv_kernel(x_ref, kvg_ref, ag_ref, wk_ref, wv_ref, wq_ref, k_ref, v_ref, kb_ref, vb_ref, qb_ref):
    x = x_ref[...]
    hk = _rms(x, kvg_ref[...]).astype(_bf16)
    k = jnp.dot(hk, wk_ref[...], preferred_element_type=_f32)
    v = jnp.dot(hk, wv_ref[...], preferred_element_type=_f32)
    k_ref[...] = k
    v_ref[...] = v
    kb_ref[...] = k.astype(_bf16)
    vb_ref[...] = v.astype(_bf16)
    ha = _rms(x, ag_ref[...]).astype(_bf16)
    q = jnp.dot(ha, wq_ref[...], preferred_element_type=_f32)
    qb_ref[...] = (q * (HEAD_DIM ** -0.5)).astype(_bf16)


def _qkv(x, kvg, ag, wk, wv, wq):
    tn = TOKEN_TILE
    row = pl.BlockSpec((tn, D_MODEL), lambda i: (i, 0))
    vec = pl.BlockSpec((1, D_MODEL), lambda i: (0, 0))
    mat = pl.BlockSpec((D_MODEL, D_MODEL), lambda i: (0, 0))
    return pl.pallas_call(
        _qkv_kernel,
        grid=(N_TOKENS // tn,),
        in_specs=[row, vec, vec, mat, mat, mat],
        out_specs=[row, row, row, row, row],
        out_shape=[jax.ShapeDtypeStruct((N_TOKENS, D_MODEL), _f32),
                   jax.ShapeDtypeStruct((N_TOKENS, D_MODEL), _f32),
                   jax.ShapeDtypeStruct((N_TOKENS, D_MODEL), _bf16),
                   jax.ShapeDtypeStruct((N_TOKENS, D_MODEL), _bf16),
                   jax.ShapeDtypeStruct((N_TOKENS, D_MODEL), _bf16)],
        compiler_params=pltpu.CompilerParams(
            dimension_semantics=("arbitrary",), vmem_limit_bytes=VMEM_LIMIT),
        name="qkv_proj",
    )(x, kvg, ag, wk, wv, wq)


def _attn_prompt_kernel(q_ref, k_ref, v_ref, lam_ref, sg_ref, o_ref, *, lam_init):
    hd = pl.program_id(1)
    slope = jnp.exp2(-(hd + 1).astype(_f32))
    lam = _diff_lambda(lam_ref, lam_init)
    tq = ATTN_Q_TILE
    lane = lax.broadcasted_iota(jnp.int32, (1, V_DIM), 1)
    first_half = lane < HEAD_DIM
    for i in range(T_PROMPT // tq):
        kv = (i + 1) * tq
        q = q_ref[i * tq:(i + 1) * tq, :]
        k = k_ref[0:kv, :]
        v = v_ref[0:kv, :]
        qpos = i * tq + lax.broadcasted_iota(jnp.int32, (tq, 1), 0)
        kpos = lax.broadcasted_iota(jnp.int32, (1, kv), 1)
        dist = (qpos - kpos).astype(_f32)
        visible = dist >= 0.0
        bias = slope * dist
        probs = []
        for c in range(2):
            qc = jnp.where(first_half if c == 0 else jnp.logical_not(first_half), q, jnp.zeros_like(q))
            s = _dot_nt(qc, k) - bias
            s = jnp.where(visible, s, NEG_INF)
            p = jnp.exp(s - jnp.max(s, axis=-1, keepdims=True))
            probs.append(p / jnp.sum(p, axis=-1, keepdims=True))
        a = probs[0] - lam * probs[1]
        o = jnp.dot(a.astype(_bf16), v, preferred_element_type=_f32)
        o = _rms(o, sg_ref[...]) * (1.0 - lam_init)
        o_ref[i * tq:(i + 1) * tq, :] = o.astype(_bf16)


def _attn_prompt(qb, kb, vb, lam_qk, sg, lam_init):
    blk = pl.BlockSpec((T_PROMPT, V_DIM), lambda b, h: (b, h))
    return pl.pallas_call(
        functools.partial(_attn_prompt_kernel, lam_init=lam_init),
        grid=(BATCH, N_HEADS),
        in_specs=[blk, blk, blk,
                  pl.BlockSpec((4, HEAD_DIM), lambda b, h: (0, 0)),
                  pl.BlockSpec((1, V_DIM), lambda b, h: (0, 0))],
        out_specs=blk,
        out_shape=jax.ShapeDtypeStruct((N_PROMPT, D_MODEL), _bf16),
        compiler_params=pltpu.CompilerParams(
            dimension_semantics=("arbitrary", "arbitrary"), vmem_limit_bytes=VMEM_LIMIT),
        name="attn_prompt",
    )(qb, kb, vb, lam_qk, sg)


def _attn_sample_kernel(pt_ref, q_ref, kn_ref, vn_ref, lam_ref, sg_ref, *rest, lam_init):
    npg = PAGES_PER_STEP
    k_refs = rest[:npg]
    v_refs = rest[npg:2 * npg]
    o_ref, q2_scr, m_scr, l_scr, acc_scr = rest[2 * npg:]
    j = pl.program_id(1)
    rows = 2 * N_HEADS
    flat = PAGE_SIZE * N_HEADS

    @pl.when(j == 0)
    def _():
        q8 = q_ref[0]
        lane = lax.broadcasted_iota(jnp.int32, (N_HEADS, V_DIM), 1)
        q2 = jnp.concatenate([jnp.where(lane < HEAD_DIM, q8, 0.0), jnp.where(lane >= HEAD_DIM, q8, 0.0)], axis=0)
        q2_scr[...] = q2.astype(_bf16)
        kn = kn_ref[0].astype(_bf16).astype(_f32)
        vn = vn_ref[0].astype(_bf16).astype(_f32)
        m_scr[...] = jnp.sum(q2 * jnp.concatenate([kn, kn], axis=0), axis=-1, keepdims=True)
        l_scr[...] = jnp.ones_like(l_scr)
        acc_scr[...] = jnp.concatenate([vn, vn], axis=0)

    q2 = q2_scr[...]
    s = jnp.concatenate(
        [_dot_nt(q2, k_refs[i][0].reshape(flat, V_DIM).astype(_bf16)) for i in range(npg)], axis=1)
    width = npg * flat
    col = lax.broadcasted_iota(jnp.int32, (1, width), 1)
    kpos = j * (npg * PAGE_SIZE) + col // N_HEADS
    dist = (PAST_LEN - kpos).astype(_f32)
    head = lax.broadcasted_iota(jnp.int32, (rows, 1), 0) % N_HEADS
    slope = jnp.exp2(-(head + 1).astype(_f32))
    s = jnp.where(col % N_HEADS == head, s - slope * dist, NEG_INF)
    m_old = m_scr[...]
    m_new = jnp.maximum(m_old, jnp.max(s, axis=-1, keepdims=True))
    corr = jnp.exp(m_old - m_new)
    p = jnp.exp(s - m_new)
    l_scr[...] = l_scr[...] * corr + jnp.sum(p, axis=-1, keepdims=True)
    pb = p.astype(_bf16)
    pv = jnp.zeros((rows, V_DIM), _f32)
    for i in range(npg):
        pv = pv + jnp.dot(pb[:, i * flat:(i + 1) * flat], v_refs[i][0].reshape(flat, V_DIM).astype(_bf16),
                          preferred_element_type=_f32)
    acc_scr[...] = acc_scr[...] * corr + pv
    m_scr[...] = m_new

    @pl.when(j == pl.num_programs(1) - 1)
    def _():
        lam = _diff_lambda(lam_ref, lam_init)
        o = acc_scr[...] / l_scr[...]
        diff = o[0:N_HEADS] - lam * o[N_HEADS:rows]
        o_ref[0] = _rms(diff, sg_ref[...]) * (1.0 - lam_init)


def _attn_sample(page_table, q_s, k_s, v_s, lam_qk, sg, cache_k, cache_v, lam_init):
    npg = PAGES_PER_STEP
    n_pages = page_table.shape[1]
    row = pl.BlockSpec((1, N_HEADS, V_DIM), lambda b, j, pt: (b, 0, 0))

    def page_spec(i):
        return pl.BlockSpec((1, PAGE_SIZE, N_HEADS, V_DIM), lambda b, j, pt: (pt[b, j * npg + i], 0, 0, 0))

    grid_spec = pltpu.PrefetchScalarGridSpec(
        num_scalar_prefetch=1,
        grid=(DEC_BATCH, n_pages // npg),
        in_specs=[row, row, row,
                  pl.BlockSpec((4, HEAD_DIM), lambda b, j, pt: (0, 0)),
                  pl.BlockSpec((1, V_DIM), lambda b, j, pt: (0, 0))]
                 + [page_spec(i) for i in range(npg)] + [page_spec(i) for i in range(npg)],
        out_specs=row,
        scratch_shapes=[pltpu.VMEM((2 * N_HEADS, V_DIM), _bf16),
                        pltpu.VMEM((2 * N_HEADS, 1), _f32),
                        pltpu.VMEM((2 * N_HEADS, 1), _f32),
                        pltpu.VMEM((2 * N_HEADS, V_DIM), _f32)],
    )
    return pl.pallas_call(
        functools.partial(_attn_sample_kernel, lam_init=lam_init),
        grid_spec=grid_spec,
        out_shape=jax.ShapeDtypeStruct((DEC_BATCH, N_HEADS, V_DIM), _f32),
        compiler_params=pltpu.CompilerParams(
            dimension_semantics=("arbitrary", "arbitrary"), vmem_limit_bytes=VMEM_LIMIT),
        name="attn_sample",
    )(page_table, q_s, k_s, v_s, lam_qk, sg, *([cache_k] * npg), *([cache_v] * npg))


def _oproj_kernel(x_ref, o_ref, w_ref, y_ref):
    y_ref[...] = x_ref[...] + jnp.dot(o_ref[...], w_ref[...], preferred_element_type=_f32)


def _oproj(x, o_bf16, w_bf16):
    tn = TOKEN_TILE
    row = pl.BlockSpec((tn, D_MODEL), lambda i: (i, 0))
    return pl.pallas_call(
        _oproj_kernel,
        grid=(N_TOKENS // tn,),
        in_specs=[row, row, pl.BlockSpec((D_MODEL, D_MODEL), lambda i: (0, 0))],
        out_specs=row,
        out_shape=jax.ShapeDtypeStruct((N_TOKENS, D_MODEL), _f32),
        compiler_params=pltpu.CompilerParams(
            dimension_semantics=("arbitrary",), vmem_limit_bytes=VMEM_LIMIT),
        name="attn_out_proj",
    )(x, o_bf16, w_bf16)


def kernel(x_prompt, x_sample, state_pool, cache_k, cache_v, page_table, meta_tokens, pool_norm, pool_w, pool_scale, kv_norm, w_k, w_v, attn_norm, w_q, lambda_qk, subln_gain, w_o, ffn_norm, peer_wq, peer_subkeys, peer_u, peer_v, final_norm):
    assert DEPTH == 2 and N_A == 1
    vec = lambda a: a.reshape(1, -1).astype(_f32)
    pad_rows = N_TOKENS - N_PROMPT - DEC_BATCH

    def peer_layer(x, l, final):
        return _peer(x, vec(ffn_norm[l]), peer_wq[l].T.astype(_bf16), peer_subkeys[l].astype(_bf16),
                     peer_u[l].astype(_bf16), peer_v[l].T.astype(_bf16), vec(final_norm), final)

    meta = jnp.broadcast_to(meta_tokens[None], (BATCH, N_META, D_MODEL))
    xp = jnp.concatenate([meta, x_prompt], axis=1)
    pw = pool_w[0].astype(_bf16)
    xp1, h_last = _pool_prompt(xp, vec(pool_norm[0]), pw, vec(pool_scale[0]))
    xs1, h_s = _pool_sample(x_sample.reshape(DEC_BATCH, D_MODEL), state_pool[0],
                            vec(pool_norm[0]), pw, vec(pool_scale[0]))
    new_pool_prompt = h_last[None, :, 1:]
    new_pool_sample = jnp.concatenate([state_pool[0][:, 1:], h_s[:, None]], axis=1)[None]

    x = jnp.concatenate([xp1.reshape(N_PROMPT, D_MODEL), xs1,
                         jnp.zeros((pad_rows, D_MODEL), _f32)], axis=0)
    x = peer_layer(x, 0, False)

    k, v, kb, vb, qb = _qkv(x, vec(kv_norm), vec(attn_norm[0]), w_k.astype(_bf16), w_v.astype(_bf16),
                            w_q[0].astype(_bf16))
    lam_init = LAMBDA_INIT[0]
    sg = vec(subln_gain[0])
    o_p = _attn_prompt(qb, kb, vb, lambda_qk[0], sg, lam_init)
    rows_s = slice(N_PROMPT, N_PROMPT + DEC_BATCH)
    k_s = k[rows_s].reshape(DEC_BATCH, N_HEADS, V_DIM)
    v_s = v[rows_s].reshape(DEC_BATCH, N_HEADS, V_DIM)
    q_s = qb[rows_s].astype(_f32).reshape(DEC_BATCH, N_HEADS, V_DIM)
    o_s = _attn_sample(page_table, q_s, k_s, v_s, lambda_qk[0], sg, cache_k, cache_v, lam_init)
    o_all = jnp.concatenate([o_p, o_s.reshape(DEC_BATCH, D_MODEL).astype(_bf16),
                             jnp.zeros((pad_rows, D_MODEL), _bf16)], axis=0)
    x = _oproj(x, o_all, w_o[0].astype(_bf16))
    y = peer_layer(x, 1, True)

    y_prompt = y[:N_PROMPT].reshape(BATCH, T_PROMPT, D_MODEL)[:, N_META:]
    y_sample = y[rows_s].reshape(DEC_BATCH, 1, D_MODEL)
    new_k_prompt = k[:N_PROMPT].reshape(BATCH, T_PROMPT, N_HEADS, 2 * HEAD_DIM)
    new_v_prompt = v[:N_PROMPT].reshape(BATCH, T_PROMPT, N_HEADS, V_DIM)
    new_k_sample = k_s.reshape(DEC_BATCH, 1, N_HEADS, 2 * HEAD_DIM)
    new_v_sample = v_s.reshape(DEC_BATCH, 1, N_HEADS, V_DIM)
    return (y_prompt, y_sample, new_pool_prompt, new_pool_sample,
            new_k_prompt, new_v_prompt, new_k_sample, new_v_sample)
```

```python
import functools
import math

import jax
import jax.numpy as jnp
from jax import lax
from jax.experimental import pallas as pl
from jax.experimental.pallas import tpu as pltpu

D_MODEL = 1024
BATCH = 8
SEQ = 2048
DEPTH = 2
DEC_BATCH = 32
PAST_LEN = 16384
PAGE_SIZE = 128
N_META = 16
N_A = DEPTH // 2
POOL_WINDOWS = (2, 4, 8, 16)
POOL_GROUP = D_MODEL // len(POOL_WINDOWS)
MAX_WIN = max(POOL_WINDOWS)
POOL_CTX = MAX_WIN - 1
HEAD_DIM = 64
V_DIM = 2 * HEAD_DIM
N_HEADS = D_MODEL // V_DIM
PEER_HEADS = 8
PEER_KEYS = 128
PEER_EXPERTS = PEER_KEYS * PEER_KEYS
PEER_TOPK = 16
PEER_QDIM = 256
PEER_HALF = PEER_QDIM // 2
EPS = 1e-6
LAMBDA_INIT = tuple(0.8 - 0.6 * math.exp(-0.3 * l) for l in range(N_A, DEPTH))

T_PROMPT = SEQ + N_META
N_PROMPT = BATCH * T_PROMPT
TOKEN_TILE = 512
N_TOKENS = -(-(N_PROMPT + DEC_BATCH) // TOKEN_TILE) * TOKEN_TILE
LANE = 128
EXPERT_CHUNK = 1024
POOL_T_TILE = 688
ATTN_Q_TILE = 344
PAGES_PER_STEP = 8
VMEM_LIMIT = 56 * 1024 * 1024
NEG_INF = float("-inf")

_f32 = jnp.float32
_bf16 = jnp.bfloat16


def _rms(x, g):
    return x * lax.rsqrt(jnp.mean(x * x, axis=-1, keepdims=True) + EPS) * g


def _gelu_tanh(x):
    return 0.5 * x * (1.0 + jnp.tanh(math.sqrt(2.0 / math.pi) * (x + 0.044715 * (x * x * x))))


def _dot_nt(a, b):
    return lax.dot_general(a, b, (((1,), (1,)), ((), ())), preferred_element_type=_f32)


def _diff_lambda(lam_ref, lam_init):
    lf = lam_ref[...]
    t1 = jnp.sum(lf[0:1] * lf[1:2], axis=-1, keepdims=True)
    t2 = jnp.sum(lf[2:3] * lf[3:4], axis=-1, keepdims=True)
    return jnp.exp(t1) - jnp.exp(t2) + lam_init


def _pool_prompt_kernel(x_ref, g_ref, w_ref, sc_ref, o_ref, hl_ref, hext):
    ti = pl.program_id(1)
    tt = POOL_T_TILE
    x = x_ref[0]
    h = _rms(x, g_ref[...])

    @pl.when(ti == 0)
    def _():
        hext[0:MAX_WIN, :] = jnp.zeros((MAX_WIN, D_MODEL), _f32)

    @pl.when(ti > 0)
    def _():
        hext[0:MAX_WIN, :] = hext[tt:tt + MAX_WIN, :]

    hext[MAX_WIN:, :] = h
    pos = ti * tt + lax.broadcasted_iota(jnp.int32, (tt, 1), 0)
    for g, w in enumerate(POOL_WINDOWS):
        c0 = g * POOL_GROUP
        win = hext[MAX_WIN:MAX_WIN + tt, c0:c0 + POOL_GROUP]
        for i in range(1, w):
            win = win + hext[MAX_WIN - i:MAX_WIN - i + tt, c0:c0 + POOL_GROUP]
        cnt = jnp.minimum(pos + 1, w).astype(_f32)
        d = win / cnt - h[:, c0:c0 + POOL_GROUP]
        y = jnp.dot(d.astype(_bf16), w_ref[g], preferred_element_type=_f32)
        o_ref[0, :, c0:c0 + POOL_GROUP] = x[:, c0:c0 + POOL_GROUP] + y * sc_ref[:, c0:c0 + POOL_GROUP]

    @pl.when(ti == pl.num_programs(1) - 1)
    def _():
        hl_ref[0] = h[tt - MAX_WIN:, :]


def _pool_prompt(x, g, w_bf16, sc):
    nt = T_PROMPT // POOL_T_TILE
    return pl.pallas_call(
        _pool_prompt_kernel,
        grid=(BATCH, nt),
        in_specs=[
            pl.BlockSpec((1, POOL_T_TILE, D_MODEL), lambda b, t: (b, t, 0)),
            pl.BlockSpec((1, D_MODEL), lambda b, t: (0, 0)),
            pl.BlockSpec((len(POOL_WINDOWS), POOL_GROUP, POOL_GROUP), lambda b, t: (0, 0, 0)),
            pl.BlockSpec((1, D_MODEL), lambda b, t: (0, 0)),
        ],
        out_specs=[
            pl.BlockSpec((1, POOL_T_TILE, D_MODEL), lambda b, t: (b, t, 0)),
            pl.BlockSpec((1, MAX_WIN, D_MODEL), lambda b, t: (b, 0, 0)),
        ],
        out_shape=[
            jax.ShapeDtypeStruct((BATCH, T_PROMPT, D_MODEL), _f32),
            jax.ShapeDtypeStruct((BATCH, MAX_WIN, D_MODEL), _f32),
        ],
        scratch_shapes=[pltpu.VMEM((POOL_T_TILE + MAX_WIN, D_MODEL), _f32)],
        compiler_params=pltpu.CompilerParams(
            dimension_semantics=("arbitrary", "arbitrary"), vmem_limit_bytes=VMEM_LIMIT),
        name="pool_prompt",
    )(x, g, w_bf16, sc)


def _pool_sample_kernel(x_ref, st_ref, g_ref, w_ref, sc_ref, o_ref, h_ref):
    x = x_ref[...]
    h = _rms(x, g_ref[...])
    h_ref[...] = h
    for g, w in enumerate(POOL_WINDOWS):
        c0 = g * POOL_GROUP
        win = h[:, c0:c0 + POOL_GROUP]
        for i in range(1, w):
            win = win + st_ref[:, POOL_CTX - i, c0:c0 + POOL_GROUP]
        d = win / float(w) - h[:, c0:c0 + POOL_GROUP]
        y = jnp.dot(d.astype(_bf16), w_ref[g], preferred_element_type=_f32)
        o_ref[:, c0:c0 + POOL_GROUP] = x[:, c0:c0 + POOL_GROUP] + y * sc_ref[:, c0:c0 + POOL_GROUP]


def _pool_sample(x, state, g, w_bf16, sc):
    return pl.pallas_call(
        _pool_sample_kernel,
        out_shape=[jax.ShapeDtypeStruct((DEC_BATCH, D_MODEL), _f32),
                   jax.ShapeDtypeStruct((DEC_BATCH, D_MODEL), _f32)],
        compiler_params=pltpu.CompilerParams(vmem_limit_bytes=VMEM_LIMIT),
        name="pool_sample",
    )(x, state, g, w_bf16, sc)


def _top16(s, with_rank):
    rows = []
    rank = jnp.full(s.shape, float(PEER_TOPK), _f32) if with_rank else None
    for i in range(PEER_TOPK):
        m = jnp.max(s, axis=0, keepdims=True)
        rows.append(m)
        hit = s == m
        if with_rank:
            rank = jnp.where(hit, float(i), rank)
        s = jnp.where(hit, NEG_INF, s)
    return rows, rank


def _peer_route_unit(h, ch, qt_scr, sk_ref, jf_scr, e1_scr, r2_scr, e2_scr):
    tok = pl.ds(pl.multiple_of(ch * LANE, LANE), LANE)
    halves = []
    for c in range(2):
        d0 = pl.multiple_of((h * 2 + c) * PEER_HALF, PEER_HALF)
        slot = d0 // EXPERT_CHUNK
        q = qt_scr[slot, pl.ds(pl.multiple_of(d0 % EXPERT_CHUNK, PEER_HALF), PEER_HALF), tok]
        s = jnp.dot(sk_ref[h, c], q.astype(_bf16), preferred_element_type=_f32)
        halves.append((s,) + _top16(s, with_rank=(c == 1)))
    (s1, a_rows, _), (s2, b_rows, r2) = halves
    a8 = jnp.concatenate(a_rows[8:16], axis=0)
    b8 = jnp.concatenate(b_rows[0:8], axis=0)
    b16 = jnp.concatenate(b_rows, axis=0)
    cands = [a_rows[0] + b16]
    cands += [a_rows[i] + b8 for i in range(1, 8)]
    cands += [a8 + b_rows[0]]
    top, _ = _top16(jnp.concatenate(cands, axis=0), with_rank=False)
    tau = top[PEER_TOPK - 1]
    z = jnp.ones_like(tau)
    for k in range(1, PEER_TOPK):
        z = z + jnp.exp(top[k] - top[0])
    jcount = jnp.zeros(s1.shape, _f32)
    for i in range(8):
        cnt = jnp.sum(jnp.where(cands[i] >= tau, 1.0, 0.0), axis=0, keepdims=True)
        jcount = jnp.where(s1 == a_rows[i], cnt, jcount)
    late = (s1 < a_rows[7]) & (s1 >= a_rows[PEER_TOPK - 1]) & (s1 + b_rows[0] >= tau)
    jcount = jnp.where(late, 1.0, jcount)
    jf_scr[h, :, tok] = jcount
    e1_scr[h, :, tok] = jnp.exp(s1 - a_rows[0]) / z
    r2_scr[h, :, tok] = r2.astype(_bf16)
    e2_scr[h, :, tok] = jnp.exp(s2 - b_rows[0]).astype(_bf16)


def _peer_gate_chunk(chunk, g_scr, p_scr, slot, jf_scr, e1_scr, r2_scr, e2_scr):
    blocks = EXPERT_CHUNK // PEER_KEYS
    a_rows = pl.ds(pl.multiple_of(chunk * blocks, blocks), blocks)
    for ch in range(TOKEN_TILE // LANE):
        tok = slice(ch * LANE, (ch + 1) * LANE)
        jg = [jf_scr[h, a_rows, tok] for h in range(PEER_HEADS)]
        eg = [e1_scr[h, a_rows, tok] for h in range(PEER_HEADS)]
        for ab in range(blocks):
            rows = slice(ab * PEER_KEYS, (ab + 1) * PEER_KEYS)
            act = _gelu_tanh(g_scr[slot, rows, tok].astype(_bf16))
            w = jnp.zeros((PEER_KEYS, LANE), _bf16)
            for h in range(PEER_HEADS):
                jrow = jnp.broadcast_to(jg[h][ab:ab + 1], (PEER_KEYS, LANE)).astype(_bf16)
                erow = jnp.broadcast_to(eg[h][ab:ab + 1], (PEER_KEYS, LANE)).astype(_bf16)
                e2 = e2_scr[h, :, tok]
                w = w + erow * jnp.where(r2_scr[h, :, tok] < jrow, e2, jnp.zeros_like(e2))
            p_scr[slot, rows, tok] = w * act


def _peer_kernel(x_ref, g_ref, wqt_ref, sk_ref, u0_ref, ua_ref, ub_ref, vta_ref, vtb_ref, vtl_ref, fn_ref, o_ref,
                 hb_scr, jf_scr, e1_scr, r2_scr, e2_scr, g_scr, p_scr, acc_scr, *, final_norm):
    c = pl.program_id(1)
    n_chunks = TOKEN_TILE // LANE

    @pl.when(c == 0)
    def _():
        hb = _rms(x_ref[...], g_ref[...]).astype(_bf16)
        hb_scr[...] = hb
        g_scr[0] = _dot_nt(wqt_ref[0:EXPERT_CHUNK, :], hb)
        g_scr[1] = _dot_nt(wqt_ref[EXPERT_CHUNK:2 * EXPERT_CHUNK, :], hb)

        def unit(u, carry):
            h = u // (n_chunks // 2)
            ch = (u % (n_chunks // 2)) * 2
            _peer_route_unit(h, ch, g_scr, sk_ref, jf_scr, e1_scr, r2_scr, e2_scr)
            _peer_route_unit(h, ch + 1, g_scr, sk_ref, jf_scr, e1_scr, r2_scr, e2_scr)
            return carry

        lax.fori_loop(0, PEER_HEADS * n_chunks // 2, unit, 0)
        g_scr[0] = _dot_nt(u0_ref[...], hb)
        p_scr[1] = jnp.zeros(p_scr.shape[1:], _bf16)
        acc_scr[...] = jnp.zeros_like(acc_scr)

    hb = hb_scr[...]
    _peer_gate_chunk(2 * c, g_scr, p_scr, 0, jf_scr, e1_scr, r2_scr, e2_scr)
    g_scr[1] = _dot_nt(ua_ref[...], hb)
    out_a = jnp.dot(vta_ref[...], p_scr[1], preferred_element_type=_f32)
    _peer_gate_chunk(2 * c + 1, g_scr, p_scr, 1, jf_scr, e1_scr, r2_scr, e2_scr)
    g_scr[0] = _dot_nt(ub_ref[...], hb)
    acc_scr[...] = acc_scr[...] + (out_a + jnp.dot(vtb_ref[...], p_scr[0], preferred_element_type=_f32))

    @pl.when(c == pl.num_programs(1) - 1)
    def _():
        acc = acc_scr[...] + jnp.dot(vtl_ref[...], p_scr[1], preferred_element_type=_f32)
        y = x_ref[...] + acc.T
        if final_norm:
            y = _rms(y, fn_ref[...])
        o_ref[...] = y


def _peer(x, g, wqt_bf16, sk_bf16, u_bf16, vt_bf16, fn, final_norm):
    tn = TOKEN_TILE
    n_ec = PEER_EXPERTS // EXPERT_CHUNK
    assert n_ec % 2 == 0 and EXPERT_CHUNK // PEER_KEYS == 8 and PEER_HEADS * PEER_QDIM == 2 * EXPERT_CHUNK
    last = n_ec - 1
    u_spec = lambda f, **kw: pl.BlockSpec((EXPERT_CHUNK, D_MODEL), lambda i, c: (f(c), 0), **kw)
    vt_spec = lambda f, **kw: pl.BlockSpec((D_MODEL, EXPERT_CHUNK), lambda i, c: (0, f(c)), **kw)
    once = dict(pipeline_mode=pl.Buffered(1))
    return pl.pallas_call(
        functools.partial(_peer_kernel, final_norm=final_norm),
        grid=(N_TOKENS // tn, n_ec // 2),
        in_specs=[
            pl.BlockSpec((tn, D_MODEL), lambda i, c: (i, 0)),
            pl.BlockSpec((1, D_MODEL), lambda i, c: (0, 0)),
            pl.BlockSpec((PEER_HEADS * PEER_QDIM, D_MODEL), lambda i, c: (0, 0), **once),
            pl.BlockSpec((PEER_HEADS, 2, PEER_KEYS, PEER_HALF), lambda i, c: (0, 0, 0, 0)),
            u_spec(lambda c: 0, **once),
            u_spec(lambda c: 2 * c + 1),
            u_spec(lambda c: jnp.minimum(2 * c + 2, last)),
            vt_spec(lambda c: jnp.maximum(2 * c - 1, 0)),
            vt_spec(lambda c: 2 * c),
            vt_spec(lambda c: last, **once),
            pl.BlockSpec((1, D_MODEL), lambda i, c: (0, 0)),
        ],
        out_specs=pl.BlockSpec((tn, D_MODEL), lambda i, c: (i, 0)),
        out_shape=jax.ShapeDtypeStruct((N_TOKENS, D_MODEL), _f32),
        scratch_shapes=[
            pltpu.VMEM((tn, D_MODEL), _bf16),
            pltpu.VMEM((PEER_HEADS, PEER_KEYS, tn), _f32),
            pltpu.VMEM((PEER_HEADS, PEER_KEYS, tn), _f32),
            pltpu.VMEM((PEER_HEADS, PEER_KEYS, tn), _bf16),
            pltpu.VMEM((PEER_HEADS, PEER_KEYS, tn), _bf16),
            pltpu.VMEM((2, EXPERT_CHUNK, tn), _f32),
            pltpu.VMEM((2, EXPERT_CHUNK, tn), _bf16),
            pltpu.VMEM((D_MODEL, tn), _f32),
        ],
        compiler_params=pltpu.CompilerParams(
            dimension_semantics=("arbitrary", "arbitrary"), vmem_limit_bytes=VMEM_LIMIT),
        name="peer_final" if final_norm else "peer",
    )(x, g, wqt_bf16, sk_bf16, u_bf16, u_bf16, u_bf16, vt_bf16, vt_bf16, vt_bf16, fn)


def _qkv_kernel(x_ref, kvg_ref, ag_ref, wk_ref, wv_ref, wq_ref, k_ref, v_ref, kb_ref, vb_ref, qb_ref):
    x = x_ref[...]
    hk = _rms(x, kvg_ref[...]).astype(_bf16)
    k = jnp.dot(hk, wk_ref[...], preferred_element_type=_f32)
    v = jnp.dot(hk, wv_ref[...], preferred_element_type=_f32)
    k_ref[...] = k
    v_ref[...] = v
    kb_ref[...] = k.astype(_bf16)
    vb_ref[...] = v.astype(_bf16)
    ha = _rms(x, ag_ref[...]).astype(_bf16)
    q = jnp.dot(ha, wq_ref[...], preferred_element_type=_f32)
    qb_ref[...] = (q * (HEAD_DIM ** -0.5)).astype(_bf16)


def _qkv(x, kvg, ag, wk, wv, wq):
    tn = TOKEN_TILE
    row = pl.BlockSpec((tn, D_MODEL), lambda i: (i, 0))
    vec = pl.BlockSpec((1, D_MODEL), lambda i: (0, 0))
    mat = pl.BlockSpec((D_MODEL, D_MODEL), lambda i: (0, 0))
    return pl.pallas_call(
        _qkv_kernel,
        grid=(N_TOKENS // tn,),
        in_specs=[row, vec, vec, mat, mat, mat],
        out_specs=[row, row, row, row, row],
        out_shape=[jax.ShapeDtypeStruct((N_TOKENS, D_MODEL), _f32),
                   jax.ShapeDtypeStruct((N_TOKENS, D_MODEL), _f32),
                   jax.ShapeDtypeStruct((N_TOKENS, D_MODEL), _bf16),
                   jax.ShapeDtypeStruct((N_TOKENS, D_MODEL), _bf16),
                   jax.ShapeDtypeStruct((N_TOKENS, D_MODEL), _bf16)],
        compiler_params=pltpu.CompilerParams(
            dimension_semantics=("arbitrary",), vmem_limit_bytes=VMEM_LIMIT),
        name="qkv_proj",
    )(x, kvg, ag, wk, wv, wq)


def _attn_prompt_kernel(q_ref, k_ref, v_ref, lam_ref, sg_ref, o_ref, *, lam_init):
    hd = pl.program_id(1)
    slope = jnp.exp2(-(hd + 1).astype(_f32))
    lam = _diff_lambda(lam_ref, lam_init)
    tq = ATTN_Q_TILE
    lane = lax.broadcasted_iota(jnp.int32, (1, V_DIM), 1)
    first_half = lane < HEAD_DIM
    for i in range(T_PROMPT // tq):
        kv = (i + 1) * tq
        q = q_ref[i * tq:(i + 1) * tq, :]
        k = k_ref[0:kv, :]
        v = v_ref[0:kv, :]
        qpos = i * tq + lax.broadcasted_iota(jnp.int32, (tq, 1), 0)
        kpos = lax.broadcasted_iota(jnp.int32, (1, kv), 1)
        dist = (qpos - kpos).astype(_f32)
        visible = dist >= 0.0
        bias = slope * dist
        probs = []
        for c in range(2):
            qc = jnp.where(first_half if c == 0 else jnp.logical_not(first_half), q, jnp.zeros_like(q))
            s = _dot_nt(qc, k) - bias
            s = jnp.where(visible, s, NEG_INF)
            p = jnp.exp(s - jnp.max(s, axis=-1, keepdims=True))
            probs.append(p / jnp.sum(p, axis=-1, keepdims=True))
        a = probs[0] - lam * probs[1]
        o = jnp.dot(a.astype(_bf16), v, preferred_element_type=_f32)
        o = _rms(o, sg_ref[...]) * (1.0 - lam_init)
        o_ref[i * tq:(i + 1) * tq, :] = o.astype(_bf16)


def _attn_prompt(qb, kb, vb, lam_qk, sg, lam_init):
    blk = pl.BlockSpec((T_PROMPT, V_DIM), lambda b, h: (b, h))
    return pl.pallas_call(
        functools.partial(_attn_prompt_kernel, lam_init=lam_init),
        grid=(BATCH, N_HEADS),
        in_specs=[blk, blk, blk,
                  pl.BlockSpec((4, HEAD_DIM), lambda b, h: (0, 0)),
                  pl.BlockSpec((1, V_DIM), lambda b, h: (0, 0))],
        out_specs=blk,
        out_shape=jax.ShapeDtypeStruct((N_PROMPT, D_MODEL), _bf16),
        compiler_params=pltpu.CompilerParams(
            dimension_semantics=("arbitrary", "arbitrary"), vmem_limit_bytes=VMEM_LIMIT),
        name="attn_prompt",
    )(qb, kb, vb, lam_qk, sg)


def _attn_sample_kernel(pt_ref, q_ref, kn_ref, vn_ref, lam_ref, sg_ref, *rest, lam_init):
    npg = PAGES_PER_STEP
    k_refs = rest[:npg]
    v_refs = rest[npg:2 * npg]
    o_ref, q2_scr, m_scr, l_scr, acc_scr = rest[2 * npg:]
    j = pl.program_id(1)
    rows = 2 * N_HEADS
    flat = PAGE_SIZE * N_HEADS

    @pl.when(j == 0)
    def _():
        q8 = q_ref[0]
        lane = lax.broadcasted_iota(jnp.int32, (N_HEADS, V_DIM), 1)
        q2 = jnp.concatenate([jnp.where(lane < HEAD_DIM, q8, 0.0), jnp.where(lane >= HEAD_DIM, q8, 0.0)], axis=0)
        q2_scr[...] = q2.astype(_bf16)
        kn = kn_ref[0].astype(_bf16).astype(_f32)
        vn = vn_ref[0].astype(_bf16).astype(_f32)
        m_scr[...] = jnp.sum(q2 * jnp.concatenate([kn, kn], axis=0), axis=-1, keepdims=True)
        l_scr[...] = jnp.ones_like(l_scr)
        acc_scr[...] = jnp.concatenate([vn, vn], axis=0)

    q2 = q2_scr[...]
    s = jnp.concatenate(
        [_dot_nt(q2, k_refs[i][0].reshape(flat, V_DIM).astype(_bf16)) for i in range(npg)], axis=1)
    width = npg * flat
    col = lax.broadcasted_iota(jnp.int32, (1, width), 1)
    kpos = j * (npg * PAGE_SIZE) + col // N_HEADS
    dist = (PAST_LEN - kpos).astype(_f32)
    head = lax.broadcasted_iota(jnp.int32, (rows, 1), 0) % N_HEADS
    slope = jnp.exp2(-(head + 1).astype(_f32))
    s = jnp.where(col % N_HEADS == head, s - slope * dist, NEG_INF)
    m_old = m_scr[...]
    m_new = jnp.maximum(m_old, jnp.max(s, axis=-1, keepdims=True))
    corr = jnp.exp(m_old - m_new)
    p = jnp.exp(s - m_new)
    l_scr[...] = l_scr[...] * corr + jnp.sum(p, axis=-1, keepdims=True)
    pb = p.astype(_bf16)
    pv = jnp.zeros((rows, V_DIM), _f32)
    for i in range(npg):
        pv = pv + jnp.dot(pb[:, i * flat:(i + 1) * flat], v_refs[i][0].reshape(flat, V_DIM).astype(_bf16),
                          preferred_element_type=_f32)
    acc_scr[...] = acc_scr[...] * corr + pv
    m_scr[...] = m_new

    @pl.when(j == pl.num_programs(1) - 1)
    def _():
        lam = _diff_lambda(lam_ref, lam_init)
        o = acc_scr[...] / l_scr[...]
        diff = o[0:N_HEADS] - lam * o[N_HEADS:rows]
        o_ref[0] = _rms(diff, sg_ref[...]) * (1.0 - lam_init)


def _attn_sample(page_table, q_s, k_s, v_s, lam_qk, sg, cache_k, cache_v, lam_init):
    npg = PAGES_PER_STEP
    n_pages = page_table.shape[1]
    row = pl.BlockSpec((1, N_HEADS, V_DIM), lambda b, j, pt: (b, 0, 0))

    def page_spec(i):
        return pl.BlockSpec((1, PAGE_SIZE, N_HEADS, V_DIM), lambda b, j, pt: (pt[b, j * npg + i], 0, 0, 0))

    grid_spec = pltpu.PrefetchScalarGridSpec(
        num_scalar_prefetch=1,
        grid=(DEC_BATCH, n_pages // npg),
        in_specs=[row, row, row,
                  pl.BlockSpec((4, HEAD_DIM), lambda b, j, pt: (0, 0)),
                  pl.BlockSpec((1, V_DIM), lambda b, j, pt: (0, 0))]
                 + [page_spec(i) for i in range(npg)] + [page_spec(i) for i in range(npg)],
        out_specs=row,
        scratch_shapes=[pltpu.VMEM((2 * N_HEADS, V_DIM), _bf16),
                        pltpu.VMEM((2 * N_HEADS, 1), _f32),
                        pltpu.VMEM((2 * N_HEADS, 1), _f32),
                        pltpu.VMEM((2 * N_HEADS, V_DIM), _f32)],
    )
    return pl.pallas_call(
        functools.partial(_attn_sample_kernel, lam_init=lam_init),
        grid_spec=grid_spec,
        out_shape=jax.ShapeDtypeStruct((DEC_BATCH, N_HEADS, V_DIM), _f32),
        compiler_params=pltpu.CompilerParams(
            dimension_semantics=("arbitrary", "arbitrary"), vmem_limit_bytes=VMEM_LIMIT),
        name="attn_sample",
    )(page_table, q_s, k_s, v_s, lam_qk, sg, *([cache_k] * npg), *([cache_v] * npg))


def _oproj_kernel(x_ref, o_ref, w_ref, y_ref):
    y_ref[...] = x_ref[...] + jnp.dot(o_ref[...], w_ref[...], preferred_element_type=_f32)


def _oproj(x, o_bf16, w_bf16):
    tn = TOKEN_TILE
    row = pl.BlockSpec((tn, D_MODEL), lambda i: (i, 0))
    return pl.pallas_call(
        _oproj_kernel,
        grid=(N_TOKENS // tn,),
        in_specs=[row, row, pl.BlockSpec((D_MODEL, D_MODEL), lambda i: (0, 0))],
        out_specs=row,
        out_shape=jax.ShapeDtypeStruct((N_TOKENS, D_MODEL), _f32),
        compiler_params=pltpu.CompilerParams(
            dimension_semantics=("arbitrary",), vmem_limit_bytes=VMEM_LIMIT),
        name="attn_out_proj",
    )(x, o_bf16, w_bf16)


def kernel(x_prompt, x_sample, state_pool, cache_k, cache_v, page_table, meta_tokens, pool_norm, pool_w, pool_scale, kv_norm, w_k, w_v, attn_norm, w_q, lambda_qk, subln_gain, w_o, ffn_norm, peer_wq, peer_subkeys, peer_u, peer_v, final_norm):
    assert DEPTH == 2 and N_A == 1
    vec = lambda a: a.reshape(1, -1).astype(_f32)
    pad_rows = N_TOKENS - N_PROMPT - DEC_BATCH

    def peer_layer(x, l, final):
        return _peer(x, vec(ffn_norm[l]), peer_wq[l].T.astype(_bf16), peer_subkeys[l].astype(_bf16),
                     peer_u[l].astype(_bf16), peer_v[l].T.astype(_bf16), vec(final_norm), final)

    meta = jnp.broadcast_to(meta_tokens[None], (BATCH, N_META, D_MODEL))
    xp = jnp.concatenate([meta, x_prompt], axis=1)
    pw = pool_w[0].astype(_bf16)
    xp1, h_last = _pool_prompt(xp, vec(pool_norm[0]), pw, vec(pool_scale[0]))
    xs1, h_s = _pool_sample(x_sample.reshape(DEC_BATCH, D_MODEL), state_pool[0],
                            vec(pool_norm[0]), pw, vec(pool_scale[0]))
    new_pool_prompt = h_last[None, :, 1:]
    new_pool_sample = jnp.concatenate([state_pool[0][:, 1:], h_s[:, None]], axis=1)[None]

    x = jnp.concatenate([xp1.reshape(N_PROMPT, D_MODEL), xs1,
                         jnp.zeros((pad_rows, D_MODEL), _f32)], axis=0)
    x = peer_layer(x, 0, False)

    k, v, kb, vb, qb = _qkv(x, vec(kv_norm), vec(attn_norm[0]), w_k.astype(_bf16), w_v.astype(_bf16),
                            w_q[0].astype(_bf16))
    lam_init = LAMBDA_INIT[0]
    sg = vec(subln_gain[0])
    o_p = _attn_prompt(qb, kb, vb, lambda_qk[0], sg, lam_init)
    rows_s = slice(N_PROMPT, N_PROMPT + DEC_BATCH)
    k_s = k[rows_s].reshape(DEC_BATCH, N_HEADS, V_DIM)
    v_s = v[rows_s].reshape(DEC_BATCH, N_HEADS, V_DIM)
    q_s = qb[rows_s].astype(_f32).reshape(DEC_BATCH, N_HEADS, V_DIM)
    o_s = _attn_sample(page_table, q_s, k_s, v_s, lambda_qk[0], sg, cache_k, cache_v, lam_init)
    o_all = jnp.concatenate([o_p, o_s.reshape(DEC_BATCH, D_MODEL).astype(_bf16),
                             jnp.zeros((pad_rows, D_MODEL), _bf16)], axis=0)
    x = _oproj(x, o_all, w_o[0].astype(_bf16))
    y = peer_layer(x, 1, True)

    y_prompt = y[:N_PROMPT].reshape(BATCH, T_PROMPT, D_MODEL)[:, N_META:]
    y_sample = y[rows_s].reshape(DEC_BATCH, 1, D_MODEL)
    new_k_prompt = k[:N_PROMPT].reshape(BATCH, T_PROMPT, N_HEADS, 2 * HEAD_DIM)
    new_v_prompt = v[:N_PROMPT].reshape(BATCH, T_PROMPT, N_HEADS, V_DIM)
    new_k_sample = k_s.reshape(DEC_BATCH, 1, N_HEADS, 2 * HEAD_DIM)
    new_v_sample = v_s.reshape(DEC_BATCH, 1, N_HEADS, V_DIM)
    return (y_prompt, y_sample, new_pool_prompt, new_pool_sample,
            new_k_prompt, new_v_prompt, new_k_sample, new_v_sample)
```
